```python
import jax, jax.numpy as jnp
from jax import lax
import numpy as np

D_MODEL = 1024
BATCH = 16
SEQ = 2048
DEPTH = 2

GRID_W = 64
CTX_LEN = 256
EPS = 1e-6
N_MOD = 9

D_FF = 2816

A_HEADS = 8
A_HEAD_DIM = 64
A_WIDTH = A_HEADS * A_HEAD_DIM
WIN_R = 8
WIN_C = 16
COL_BLOCK = 16
COL_BAND = COL_BLOCK + WIN_C

POOL_WINDOWS = (2, 4, 8, 16)
B_WIDTH = D_MODEL - A_WIDTH
B_GROUP = B_WIDTH // len(POOL_WINDOWS)

HG_EXPAND = 128
HG_HEADS = D_MODEL // HG_EXPAND
HG_DK = HG_EXPAND
HG_DV = D_MODEL // HG_HEADS
HG_KDIM = HG_HEADS * HG_DK
HG_VDIM = HG_HEADS * HG_DV
HG_CHUNK = 32

kernel_name = "hybrid_natten_pool_hgrn2_diffusion_block"


def rms_norm(x, g):
    xf = x.astype(jnp.float32)
    y = xf * lax.rsqrt(jnp.mean(xf * xf, axis=-1, keepdims=True) + EPS)
    return (y * g.astype(jnp.float32)).astype(x.dtype)


def adaln_in(h, g, shift, scale):
    return rms_norm(h, g) * (1 + scale) + shift


def swiglu(h, w13, w2):
    a, b = jnp.split(h @ w13, 2, axis=-1)
    return (jax.nn.silu(a) * b) @ w2


def neighbourhood_attention(q, k, v, k_ctx, v_ctx, rpb):
    bsz, n, nh, dh = q.shape
    rows = n // GRID_W
    kr = min(WIN_R, rows)
    ncb = GRID_W // COL_BLOCK
    scale = dh ** -0.5
    qg = q.reshape(bsz, rows, ncb, COL_BLOCK, nh, dh)
    kg = k.reshape(bsz, rows, GRID_W, nh, dh)
    vg = v.reshape(bsz, rows, GRID_W, nh, dh)
    col_q = jnp.arange(GRID_W).reshape(ncb, COL_BLOCK)
    win_c0 = jnp.clip(col_q - WIN_C // 2, 0, GRID_W - WIN_C)
    band0 = jnp.clip(jnp.arange(ncb) * COL_BLOCK - WIN_C // 2, 0, GRID_W - COL_BAND)
    key_col = band0[:, None] + jnp.arange(COL_BAND)
    kc = key_col[:, None, :]
    col_ok = (kc >= win_c0[:, :, None]) & (kc < win_c0[:, :, None] + WIN_C)
    rel_c = jnp.clip(kc - col_q[:, :, None], -(WIN_C - 1), WIN_C - 1) + (WIN_C - 1)
    rpb_c = rpb[:, :, rel_c]
    nloc = kr * COL_BAND

    def row_block(r):
        r0 = jnp.clip(r - kr // 2, 0, rows - kr)
        key_rows = r0 + jnp.arange(kr)
        k_band = jnp.take(kg, key_rows, axis=1)[:, :, key_col]
        v_band = jnp.take(vg, key_rows, axis=1)[:, :, key_col]
        q_blk = lax.dynamic_index_in_dim(qg, r, axis=1, keepdims=False)
        s_loc = jnp.einsum('bnqhd,bjnkhd->bhnqjk', q_blk, k_band).astype(jnp.float32) * scale
        bias = rpb_c[:, key_rows - r + (WIN_R - 1)].transpose(0, 2, 3, 1, 4)
        s_loc = s_loc + bias[None].astype(jnp.float32)
        s_loc = jnp.where(col_ok[:, :, None, :], s_loc, -jnp.inf)
        s_loc = s_loc.reshape(bsz, nh, ncb, COL_BLOCK, nloc)
        s_ctx = jnp.einsum('bnqhd,bchd->bhnqc', q_blk, k_ctx).astype(jnp.float32) * scale
        p = jax.nn.softmax(jnp.concatenate([s_loc, s_ctx], axis=-1), axis=-1).astype(v.dtype)
        p_loc = p[..., :nloc].reshape(bsz, nh, ncb, COL_BLOCK, kr, COL_BAND)
        p_ctx = p[..., nloc:]
        return (jnp.einsum('bhnqjk,bjnkhd->bnqhd', p_loc, v_band)
                + jnp.einsum('bhnqc,bchd->bnqhd', p_ctx, v_ctx))

    out = lax.map(row_block, jnp.arange(rows))
    return out.transpose(1, 0, 2, 3, 4, 5).reshape(bsz, n, nh * dh)


def context_attention(q, k, v):
    bsz, L, nh, dh = q.shape
    s = jnp.einsum('bqhd,bkhd->bhqk', q, k).astype(jnp.float32) * (dh ** -0.5)
    p = jax.nn.softmax(s, axis=-1).astype(v.dtype)
    return jnp.einsum('bhqk,bkhd->bqhd', p, v).reshape(bsz, L, nh * dh)


def centred_pool_minus_self(u, w):
    bsz, L, ch = u.shape
    t = jnp.arange(L)
    lo = jnp.clip(t - w // 2, 0, L)
    hi = jnp.clip(t - w // 2 + w, 0, L)
    uf = u.astype(jnp.float32)
    cs = jnp.concatenate([jnp.zeros((bsz, 1, ch), jnp.float32), jnp.cumsum(uf, axis=1)], axis=1)
    mean = (cs[:, hi] - cs[:, lo]) / (hi - lo).astype(jnp.float32)[None, :, None]
    return (mean - uf).astype(u.dtype)


def pool_mixer(u, pool_w, pool_scale):
    bsz, L, _ = u.shape
    groups = jnp.split(u, len(POOL_WINDOWS), axis=-1)
    d = jnp.stack([centred_pool_minus_self(gi, w) for gi, w in zip(groups, POOL_WINDOWS)], axis=2)
    y = jnp.einsum('blgc,gcd->blgd', d, pool_w).reshape(bsz, L, B_WIDTH)
    return y * pool_scale


def ab_mixer(x_lat, x_ctx, w_in, rpb, pool_w, pool_scale, w_out, need_ctx_out):
    def split(p):
        bsz, L, _ = p.shape
        q, k, v, u = jnp.split(p, [A_WIDTH, 2 * A_WIDTH, 3 * A_WIDTH], axis=-1)
        heads = lambda t: t.reshape(bsz, L, A_HEADS, A_HEAD_DIM)
        return heads(q), heads(k), heads(v), u

    q_l, k_l, v_l, u_l = split(x_lat @ w_in)
    q_c, k_c, v_c, u_c = split(x_ctx @ w_in)
    a_l = neighbourhood_attention(q_l, k_l, v_l, k_c, v_c, rpb)
    y_l = jnp.concatenate([a_l, pool_mixer(u_l, pool_w, pool_scale)], axis=-1) @ w_out
    if not need_ctx_out:
        return y_l, None
    a_c = context_attention(q_c, k_c, v_c)
    y_c = jnp.concatenate([a_c, pool_mixer(u_c, pool_w, pool_scale)], axis=-1) @ w_out
    return y_l, y_c


def gla_chunk_scan(q, k, v, logf, s0, reverse):
    if reverse:
        q, k, v, logf = (jnp.flip(t, axis=1) for t in (q, k, v, logf))
    bsz, L, nh, _ = q.shape
    nc = L // HG_CHUNK
    to_chunks = lambda t: t.reshape(bsz, nc, HG_CHUNK, nh, t.shape[-1]).transpose(1, 0, 3, 2, 4)
    causal = jnp.tril(jnp.ones((HG_CHUNK, HG_CHUNK), dtype=bool))[:, :, None]

    def step(s, xs):
        qi, ki, vi, gi = xs
        G = jnp.cumsum(gi, axis=2)
        o_inter = jnp.einsum('bhtk,bhkv->bhtv', qi * jnp.exp(G), s)
        diff = G[:, :, :, None, :] - G[:, :, None, :, :]
        decay = jnp.exp(jnp.where(causal, diff, -jnp.inf))
        a = jnp.einsum('bhtk,bhsk,bhtsk->bhts', qi, ki, decay)
        o = o_inter + jnp.einsum('bhts,bhsv->bhtv', a, vi)
        g_last = G[:, :, -1:, :]
        s_new = (jnp.exp(g_last[:, :, 0, :, None]) * s
                 + jnp.einsum('bhsk,bhsv->bhkv', ki * jnp.exp(g_last - G), vi))
        return s_new, o

    s_fin, o = lax.scan(step, s0, (to_chunks(q), to_chunks(k), to_chunks(v), to_chunks(logf)))
    o = o.transpose(1, 0, 3, 2, 4).reshape(bsz, L, nh, HG_DV)
    if reverse:
        o = jnp.flip(o, axis=1)
    return o, s_fin


def hg_project(x, w_in, lower_bounds):
    bsz, L, _ = x.shape
    p = (x @ w_in).astype(jnp.float32)
    q, i, f_fw, f_bw, g = jnp.split(
        p, [HG_KDIM, HG_KDIM + HG_VDIM, 2 * HG_KDIM + HG_VDIM, 3 * HG_KDIM + HG_VDIM], axis=-1)
    hk = lambda t: t.reshape(bsz, L, HG_HEADS, HG_DK)
    lb = lower_bounds.astype(jnp.float32)
    forget = [hk(lb[d] + (1.0 - lb[d]) * jax.nn.sigmoid(f)) for d, f in enumerate((f_fw, f_bw))]
    return hk(jax.nn.silu(q)), i.reshape(bsz, L, HG_HEADS, HG_DV), forget, g


def hg_readout(o, g, gnorm, w_out, dtype):
    bsz, L = o.shape[:2]
    o = o * lax.rsqrt(jnp.mean(o * o, axis=-1, keepdims=True) + EPS) * gnorm.astype(jnp.float32)
    o = o.reshape(bsz, L, HG_VDIM) * jax.nn.silu(g)
    return o.astype(dtype) @ w_out


def hg_mixer(x_lat, x_ctx, w_in, lower_bounds, gnorm, w_out, need_ctx_out):
    q_l, i_l, f_l, g_l = hg_project(x_lat, w_in, lower_bounds)
    q_c, i_c, f_c, g_c = hg_project(x_ctx, w_in, lower_bounds)
    bsz = x_lat.shape[0]
    outs_l, outs_c = [], []
    for d in range(2):
        rev = d == 1
        s0 = jnp.zeros((bsz, HG_HEADS, HG_DK, HG_DV), jnp.float32)
        o_c, s_ctx = gla_chunk_scan(q_c, 1.0 - f_c[d], i_c, jnp.log(f_c[d]), s0, rev)
        o_l, _ = gla_chunk_scan(q_l, 1.0 - f_l[d], i_l, jnp.log(f_l[d]), s_ctx, rev)
        outs_l.append(o_l)
        outs_c.append(o_c)
    y_l = hg_readout(outs_l[0] + outs_l[1], g_l, gnorm, w_out, x_lat.dtype)
    if not need_ctx_out:
        return y_l, None
    return y_l, hg_readout(outs_c[0] + outs_c[1], g_c, gnorm, w_out, x_ctx.dtype)


def setup_inputs(seed: int = 0) -> dict:
    key = jax.random.key(seed)
    ks = jax.random.split(key, 19)
    D = D_MODEL
    n_even = (DEPTH + 1) // 2
    n_odd = DEPTH // 2
    f32 = jnp.float32
    nrm = lambda k, shape: jax.random.normal(k, shape, f32)
    return {
        "x": nrm(ks[0], (BATCH, SEQ, D)),
        "c": nrm(ks[1], (BATCH, D)),
        "ctx": nrm(ks[2], (BATCH, CTX_LEN, D)),
        "c_ctx": nrm(ks[3], (D,)),
        "w_mod": nrm(ks[4], (DEPTH, D, N_MOD * D)) * (0.5 * D ** -0.5),
        "b_mod": nrm(ks[5], (DEPTH, N_MOD * D)) * 0.02,
        "norm_g": 1.0 + 0.1 * nrm(ks[6], (DEPTH, 3, D)),
        "ffn_w13": nrm(ks[7], (DEPTH, 2, D, 2 * D_FF)) * D ** -0.5,
        "ffn_w2": nrm(ks[8], (DEPTH, 2, D_FF, D)) * D_FF ** -0.5,
        "ab_w_in": nrm(ks[9], (n_even, D, 3 * A_WIDTH + B_WIDTH)) * D ** -0.5,
        "ab_rpb": 0.5 * nrm(ks[10], (n_even, A_HEADS, 2 * WIN_R - 1, 2 * WIN_C - 1)),
        "ab_pool_w": nrm(ks[11], (n_even, len(POOL_WINDOWS), B_GROUP, B_GROUP)) * B_GROUP ** -0.5,
        "ab_pool_scale": 1.0 + 0.1 * nrm(ks[12], (n_even, B_WIDTH)),
        "ab_w_out": nrm(ks[13], (n_even, A_WIDTH + B_WIDTH, D)) * (A_WIDTH + B_WIDTH) ** -0.5,
        "hg_w_in": nrm(ks[14], (n_odd, D, 3 * HG_KDIM + 2 * HG_VDIM)) * D ** -0.5,
        "hg_lb_logits": 0.5 * nrm(ks[15], (DEPTH, 2, HG_KDIM)),
        "hg_gnorm": 1.0 + 0.1 * nrm(ks[16], (n_odd, HG_DV)),
        "hg_w_out": nrm(ks[17], (n_odd, HG_VDIM, D)) * HG_VDIM ** -0.5,
        "final_g": 1.0 + 0.1 * nrm(ks[18], (D,)),
    }


def reference(x, c, ctx, c_ctx, w_mod, b_mod, norm_g, ffn_w13, ffn_w2, ab_w_in, ab_rpb, ab_pool_w,
              ab_pool_scale, ab_w_out, hg_w_in, hg_lb_logits, hg_gnorm, hg_w_out, final_g):
    sm = jax.nn.softmax(hg_lb_logits.astype(jnp.float32), axis=0)
    lb_all = jnp.cumsum(sm, axis=0) - sm[0:1]
    silu_c = jax.nn.silu(c)
    silu_cc = jax.nn.silu(c_ctx)
    h_lat, h_ctx = x, ctx
    for l in range(DEPTH):
        last = l == DEPTH - 1
        ml = jnp.split((silu_c @ w_mod[l] + b_mod[l])[:, None, :], N_MOD, axis=-1)
        mc = jnp.split(silu_cc @ w_mod[l] + b_mod[l], N_MOD, axis=-1)
        h_lat = h_lat + 0.5 * ml[2] * swiglu(adaln_in(h_lat, norm_g[l, 0], ml[0], ml[1]), ffn_w13[l, 0], ffn_w2[l, 0])
        h_ctx = h_ctx + 0.5 * mc[2] * swiglu(adaln_in(h_ctx, norm_g[l, 0], mc[0], mc[1]), ffn_w13[l, 0], ffn_w2[l, 0])
        xl = adaln_in(h_lat, norm_g[l, 1], ml[3], ml[4])
        xc = adaln_in(h_ctx, norm_g[l, 1], mc[3], mc[4])
        j = l // 2
        if l % 2 == 0:
            y_lat, y_ctx = ab_mixer(xl, xc, ab_w_in[j], ab_rpb[j], ab_pool_w[j], ab_pool_scale[j], ab_w_out[j],
                                    not last)
        else:
            y_lat, y_ctx = hg_mixer(xl, xc, hg_w_in[j], lb_all[l], hg_gnorm[j], hg_w_out[j], not last)
        h_lat = h_lat + ml[5] * y_lat
        h_lat = h_lat + 0.5 * ml[8] * swiglu(adaln_in(h_lat, norm_g[l, 2], ml[6], ml[7]), ffn_w13[l, 1], ffn_w2[l, 1])
        if not last:
            h_ctx = h_ctx + mc[5] * y_ctx
            h_ctx = h_ctx + 0.5 * mc[8] * swiglu(adaln_in(h_ctx, norm_g[l, 2], mc[6], mc[7]), ffn_w13[l, 1],
                                                 ffn_w2[l, 1])
    return rms_norm(h_lat, final_g)
```

```python
import functools

import jax
import jax.numpy as jnp
from jax import lax
from jax.experimental import pallas as pl
from jax.experimental.pallas import tpu as pltpu

D = 1024
BATCH = 16
SEQ = 2048
DEPTH = 2
GRID_W = 64
CTX = 256
EPS = 1e-6
N_MOD = 9
D_FF = 2816
A_HEADS = 8
A_DH = 64
A_WIDTH = 512
WIN_R = 8
WIN_C = 16
POOL_WINDOWS = (2, 4, 8, 16)
B_WIDTH = 512
B_GROUP = 128
HG_HEADS = 8
HG_DK = 128
HG_KDIM = 1024

ROWS = SEQ // GRID_W
MOD_ROWS = 24
ROW_TILE = 1024
FF_CHUNK = 256
HG_CHUNK = 128
HG_COLS = 5 * HG_DK
NEG_BIG = -1e30
VMEM_LIMIT = 56 * 1024 * 1024

F32 = jnp.float32
BF16 = jnp.bfloat16


def _dot(a, b):
    return jnp.dot(a, b, preferred_element_type=F32)


def _dot_nt(a, b):
    return lax.dot_general(a, b, (((1,), (1,)), ((), ())), preferred_element_type=F32)


def _dot_tn(a, b):
    return lax.dot_general(a, b, (((0,), (0,)), ((), ())), preferred_element_type=F32)


def _sigmoid(x):
    return 1.0 / (1.0 + jnp.exp(-x))


def _silu(x):
    return x * _sigmoid(x)


def _params(n_axes):
    return pltpu.CompilerParams(dimension_semantics=("arbitrary",) * n_axes, vmem_limit_bytes=VMEM_LIMIT)


def _const_spec(shape):
    nd = len(shape)
    return pl.BlockSpec(shape, lambda *_: (0,) * nd, pipeline_mode=pl.Buffered(1))


def _adaln(x, g, shift, scale):
    ms = jnp.mean(x * x, axis=-1, keepdims=True)
    return (x * lax.rsqrt(ms + EPS)) * g * (1.0 + scale) + shift


def _mod_kernel(c_ref, w_ref, b_ref, o_ref):
    s = _silu(c_ref[...]).astype(BF16)
    o_ref[...] = _dot(s, w_ref[...].astype(BF16)) + b_ref[...]


def _modulation(cvec, w_mod, b_mod):
    nblk = N_MOD
    out = pl.pallas_call(
        _mod_kernel,
        grid=(DEPTH, nblk),
        in_specs=[
            pl.BlockSpec((MOD_ROWS, D), lambda l, j: (0, 0)),
            pl.BlockSpec((None, D, D), lambda l, j: (l, 0, j)),
            pl.BlockSpec((None, 1, D), lambda l, j: (l, 0, j)),
        ],
        out_specs=pl.BlockSpec((None, MOD_ROWS, D), lambda l, j: (l, 0, j)),
        out_shape=jax.ShapeDtypeStruct((DEPTH, MOD_ROWS, N_MOD * D), F32),
        compiler_params=_params(2),
        name="modulation",
    )(cvec, w_mod, b_mod.reshape(DEPTH, 1, N_MOD * D))
    return out.reshape(DEPTH, MOD_ROWS, N_MOD, D)


def _mod_spec(layer, is_ctx):
    tiles_per_batch = SEQ // ROW_TILE
    if is_ctx:
        return pl.BlockSpec((None, None, N_MOD, D), lambda t: (layer, 0, 0, 0))
    return pl.BlockSpec((None, None, N_MOD, D), lambda t: (layer, 1 + t // tiles_per_batch, 0, 0))


def _norm_spec(layer):
    return pl.BlockSpec((None, 3, D), lambda t: (layer, 0, 0))


def _row_spec(width):
    return pl.BlockSpec((ROW_TILE, width), lambda t: (t, 0))


def _ffn_kernel(sub, final, h_ref, mod_ref, g_ref, w13_ref, w2_ref, *rest):
    if final:
        fg_ref, o_ref = rest
    else:
        (o_ref,) = rest
    x = h_ref[...]
    shift = mod_ref[3 * sub:3 * sub + 1, :]
    scale = mod_ref[3 * sub + 1:3 * sub + 2, :]
    gate = mod_ref[3 * sub + 2:3 * sub + 3, :]
    xb = _adaln(x, g_ref[sub:sub + 1, :], shift, scale).astype(BF16)
    acc = jnp.zeros((ROW_TILE, D), F32)
    for c in range(D_FF // FF_CHUNK):
        lo = c * FF_CHUNK
        a = _dot(xb, w13_ref[:, lo:lo + FF_CHUNK])
        b = _dot(xb, w13_ref[:, D_FF + lo:D_FF + lo + FF_CHUNK])
        act = (_silu(a) * b).astype(BF16)
        acc = acc + _dot(act, w2_ref[lo:lo + FF_CHUNK, :])
    y = x + (0.5 * gate) * acc
    if final:
        ms = jnp.mean(y * y, axis=-1, keepdims=True)
        y = (y * lax.rsqrt(ms + EPS)) * fg_ref[...]
    o_ref[...] = y


def _ffn(h, mod, norm_g, w13, w2, layer, sub, is_ctx, final_g=None):
    n = h.shape[0]
    final = final_g is not None
    in_specs = [
        _row_spec(D),
        _mod_spec(layer, is_ctx),
        _norm_spec(layer),
        _const_spec((D, 2 * D_FF)),
        _const_spec((D_FF, D)),
    ]
    args = [h, mod, norm_g, w13, w2]
    if final:
        in_specs.append(pl.BlockSpec((1, D), lambda t: (0, 0)))
        args.append(final_g.reshape(1, D))
    return pl.pallas_call(
        functools.partial(_ffn_kernel, sub, final),
        grid=(n // ROW_TILE,),
        in_specs=in_specs,
        out_specs=_row_spec(D),
        out_shape=jax.ShapeDtypeStruct((n, D), F32),
        compiler_params=_params(1),
        name="ffn",
    )(*args)


def _abproj_kernel(h_ref, mod_ref, g_ref, w_ref, q_ref, k_ref, v_ref, u_ref):
    xb = _adaln(h_ref[...], g_ref[1:2, :], mod_ref[3:4, :], mod_ref[4:5, :]).astype(BF16)
    q_ref[...] = (_dot(xb, w_ref[:, 0:A_WIDTH]) * (A_DH ** -0.5)).astype(BF16)
    k_ref[...] = _dot(xb, w_ref[:, A_WIDTH:2 * A_WIDTH]).astype(BF16)
    v_ref[...] = _dot(xb, w_ref[:, 2 * A_WIDTH:3 * A_WIDTH]).astype(BF16)
    u_ref[...] = _dot(xb, w_ref[:, 3 * A_WIDTH:])


def _abproj(h, mod, norm_g, w_in, layer, is_ctx):
    n = h.shape[0]
    return pl.pallas_call(
        _abproj_kernel,
        grid=(n // ROW_TILE,),
        in_specs=[_row_spec(D), _mod_spec(layer, is_ctx), _norm_spec(layer), _const_spec((D, 3 * A_WIDTH + B_WIDTH))],
        out_specs=[_row_spec(A_WIDTH), _row_spec(A_WIDTH), _row_spec(A_WIDTH), _row_spec(B_WIDTH)],
        out_shape=[
            jax.ShapeDtypeStruct((n, A_WIDTH), BF16),
            jax.ShapeDtypeStruct((n, A_WIDTH), BF16),
            jax.ShapeDtypeStruct((n, A_WIDTH), BF16),
            jax.ShapeDtypeStruct((n, B_WIDTH), F32),
        ],
        compiler_params=_params(1),
        name="ab_proj",
    )(h, mod, norm_g, w_in)


def _attn_bias_table(rpb):
    col = jnp.arange(GRID_W)
    win_c0 = jnp.clip(col - WIN_C // 2, 0, GRID_W - WIN_C)
    kc = jnp.arange(GRID_W)
    col_ok = (kc[None, :] >= win_c0[:, None]) & (kc[None, :] < win_c0[:, None] + WIN_C)
    rel_c = jnp.clip(kc[None, :] - col[:, None], -(WIN_C - 1), WIN_C - 1) + (WIN_C - 1)
    rel_r = jnp.arange(WIN_R)[None, :] - jnp.arange(WIN_R)[:, None] + (WIN_R - 1)
    tab = rpb[:, rel_r][:, :, :, rel_c]
    tab = jnp.where(col_ok[None, None, None], tab.astype(F32), NEG_BIG)
    tab = tab.transpose(1, 0, 3, 2, 4)
    return tab.reshape(WIN_R, A_HEADS, GRID_W, WIN_R * GRID_W)


def _softmax_pv(s_parts, v_parts):
    m = s_parts[0].max(axis=-1, keepdims=True)
    for s in s_parts[1:]:
        m = jnp.maximum(m, s.max(axis=-1, keepdims=True))
    den = None
    out = None
    for s, v in zip(s_parts, v_parts):
        p = jnp.exp(s - m)
        ps = p.sum(axis=-1, keepdims=True)
        den = ps if den is None else den + ps
        pv = _dot(p.astype(BF16), v)
        out = pv if out is None else out + pv
    return out * (1.0 / den)


def _shift_rows(x, k, row, n):
    if k > 0:
        return jnp.where(row >= k, pltpu.roll(x, k, 0), 0.0)
    return jnp.where(row < n + k, pltpu.roll(x, n + k, 0), 0.0)


def _pool_mix(u_ref, pw_ref, ps_ref, o_ref, n):
    row = lax.broadcasted_iota(jnp.int32, (n, B_GROUP), 0)
    rowf = row.astype(F32)
    for gi, w in enumerate(POOL_WINDOWS):
        x = u_ref[:, gi * B_GROUP:(gi + 1) * B_GROUP]
        half = w // 2
        back = x
        fwd = x
        m = 1
        while m < half:
            back = back + _shift_rows(back, m, row, n)
            fwd = fwd + _shift_rows(fwd, -m, row, n)
            m *= 2
        total = _shift_rows(back, 1, row, n) + fwd
        cnt = jnp.minimum(rowf + half, float(n)) - jnp.maximum(rowf - half, 0.0)
        d = (total / cnt - x).astype(BF16)
        y = _dot(d, pw_ref[gi]) * ps_ref[:, gi * B_GROUP:(gi + 1) * B_GROUP]
        o_ref[:, A_WIDTH + gi * B_GROUP:A_WIDTH + (gi + 1) * B_GROUP] = y.astype(o_ref.dtype)


def _ab_core_kernel(qc_ref, kc_ref, vc_ref, uc_ref, ql_ref, kl_ref, vl_ref, ul_ref, bias_ref, pw_ref, ps_ref,
                    oc_ref, ol_ref):
    lane = lax.broadcasted_iota(jnp.int32, (1, 2 * A_DH), 1)
    first = lane < A_DH
    zero = jnp.zeros((), BF16)

    def head_pair_cols(p):
        return slice(p * 2 * A_DH, (p + 1) * 2 * A_DH)

    for p in range(A_HEADS // 2):
        cols = head_pair_cols(p)
        q2, k2, v2 = qc_ref[:, cols], kc_ref[:, cols], vc_ref[:, cols]
        outs = []
        for e in range(2):
            qm = jnp.where(first if e == 0 else ~first, q2, zero)
            outs.append(_softmax_pv([_dot_nt(qm, k2)], [v2]))
        oc_ref[:, cols] = jnp.where(first, outs[0], outs[1]).astype(oc_ref.dtype)

    def row_body(r, carry):
        r0 = jnp.clip(r - WIN_R // 2, 0, ROWS - WIN_R)
        dr = r - r0
        qrows = pl.ds(pl.multiple_of(r * GRID_W, GRID_W), GRID_W)
        krows = pl.ds(pl.multiple_of(r0 * GRID_W, GRID_W), WIN_R * GRID_W)
        for p in range(A_HEADS // 2):
            cols = head_pair_cols(p)
            q2 = ql_ref[qrows, cols]
            k2 = kl_ref[krows, cols]
            v2 = vl_ref[krows, cols]
            kc2 = kc_ref[:, cols]
            vc2 = vc_ref[:, cols]
            outs = []
            for e in range(2):
                qm = jnp.where(first if e == 0 else ~first, q2, zero)
                s_loc = _dot_nt(qm, k2) + bias_ref[dr, 2 * p + e]
                s_ctx = _dot_nt(qm, kc2)
                outs.append(_softmax_pv([s_loc, s_ctx], [v2, vc2]))
            ol_ref[qrows, cols] = jnp.where(first, outs[0], outs[1]).astype(ol_ref.dtype)
        return carry

    lax.fori_loop(0, ROWS, row_body, 0)

    _pool_mix(uc_ref, pw_ref, ps_ref, oc_ref, CTX)
    _pool_mix(ul_ref, pw_ref, ps_ref, ol_ref, SEQ)


def _ab_core(qc, kc, vc, uc, ql, kl, vl, ul, bias, pool_w, pool_scale):
    cspec = lambda w: pl.BlockSpec((CTX, w), lambda b: (b, 0))
    lspec = lambda w: pl.BlockSpec((SEQ, w), lambda b: (b, 0))
    return pl.pallas_call(
        _ab_core_kernel,
        grid=(BATCH,),
        in_specs=[
            cspec(A_WIDTH), cspec(A_WIDTH), cspec(A_WIDTH), cspec(B_WIDTH),
            lspec(A_WIDTH), lspec(A_WIDTH), lspec(A_WIDTH), lspec(B_WIDTH),
            _const_spec((WIN_R, A_HEADS, GRID_W, WIN_R * GRID_W)),
            _const_spec((len(POOL_WINDOWS), B_GROUP, B_GROUP)),
            _const_spec((1, B_WIDTH)),
        ],
        out_specs=[cspec(D), lspec(D)],
        out_shape=[
            jax.ShapeDtypeStruct((BATCH * CTX, D), BF16),
            jax.ShapeDtypeStruct((BATCH * SEQ, D), BF16),
        ],
        compiler_params=_params(1),
        name="ab_core",
    )(qc, kc, vc, uc, ql, kl, vl, ul, bias, pool_w, pool_scale)


def _outproj_kernel(x_ref, h_ref, mod_ref, w_ref, o_ref):
    o_ref[...] = h_ref[...] + mod_ref[5:6, :] * _dot(x_ref[...], w_ref[...])


def _outproj(x, h, mod, w_out, layer, is_ctx):
    n = h.shape[0]
    return pl.pallas_call(
        _outproj_kernel,
        grid=(n // ROW_TILE,),
        in_specs=[_row_spec(D), _row_spec(D), _mod_spec(layer, is_ctx), _const_spec((D, D))],
        out_specs=_row_spec(D),
        out_shape=jax.ShapeDtypeStruct((n, D), F32),
        compiler_params=_params(1),
        name="mixer_out",
    )(x, h, mod, w_out)


def _adaln_kernel(h_ref, mod_ref, g_ref, o_ref):
    o_ref[...] = _adaln(h_ref[...], g_ref[1:2, :], mod_ref[3:4, :], mod_ref[4:5, :]).astype(o_ref.dtype)


def _adaln_rows(h, mod, norm_g, layer, is_ctx):
    n = h.shape[0]
    return pl.pallas_call(
        _adaln_kernel,
        grid=(n // ROW_TILE,),
        in_specs=[_row_spec(D), _mod_spec(layer, is_ctx), _norm_spec(layer)],
        out_specs=_row_spec(D),
        out_shape=jax.ShapeDtypeStruct((n, D), BF16),
        compiler_params=_params(1),
        name="adaln",
    )(h, mod, norm_g)


def _bcast_row(x3, k):
    return jnp.broadcast_to(x3[:, k:k + 1, :], x3.shape)


def _level_refs(g, reverse):
    ngrp = HG_CHUNK // 8
    g3 = g.reshape(ngrp, 8, HG_DK)
    sub = lax.broadcasted_iota(jnp.int32, g3.shape, 1)
    pick = (lambda k: k + 1) if reverse else (lambda k: k)
    refs = []
    r = [_bcast_row(g3, pick(k)) for k in (0, 2, 4, 6)]
    refs.append(jnp.where(sub < 2, r[0], jnp.where(sub < 4, r[1], jnp.where(sub < 6, r[2], r[3]))))
    r = [_bcast_row(g3, pick(k)) for k in (1, 5)]
    refs.append(jnp.where(sub < 4, r[0], r[1]))
    refs.append(_bcast_row(g3, pick(3)))
    edge = _bcast_row(g3, 0 if reverse else 7)
    for c in (8, 16, 32, 64):
        span = 2 * c // 8
        parts = []
        for blk in range(ngrp // span):
            src = blk * span + span // 2 - (0 if reverse else 1)
            parts.extend([edge[src:src + 1]] * span)
        refs.append(jnp.concatenate(parts, axis=0))
    return [x.reshape(HG_CHUNK, HG_DK) for x in refs]


def _hg_chunk(q, k, v, g, st, level_of, reverse, want_out):
    g_end = g[0:1, :] if reverse else g[HG_CHUNK - 1:HG_CHUNK, :]
    vb = v.astype(BF16)
    kp = (k * jnp.exp(g_end - g)).astype(BF16)
    st_new = st * jnp.exp(g_end) + _dot_tn(vb, kp)
    if not want_out:
        return None, st_new
    qp = (q * jnp.exp(g)).astype(BF16)
    o = _dot_nt(qp, st.astype(BF16))
    row = lax.broadcasted_iota(jnp.int32, (HG_CHUNK, HG_CHUNK), 0)
    colm = lax.broadcasted_iota(jnp.int32, (HG_CHUNK, HG_CHUNK), 1)
    a = jnp.where(row == colm, jnp.sum(q * k, axis=-1, keepdims=True), 0.0)
    for lvl, ref in enumerate(_level_refs(g, reverse)):
        w = jnp.exp(-jnp.abs(g - ref))
        a_l = _dot_nt((q * w).astype(BF16), (k * w).astype(BF16))
        a = jnp.where(level_of == lvl, a_l, a)
    return o + _dot(a.astype(BF16), vb), st_new


def _hg_kernel(xc_ref, xl_ref, w_ref, lb_ref, gn_ref, o_ref,
               q_s, v_s, kf_s, kb_s, gf_s, gb_s, of_s, ob_s):
    n_ctx_chunks = CTX // HG_CHUNK
    n_lat_chunks = SEQ // HG_CHUNK
    lb = lb_ref[...]

    def project(x_ref, base, n):
        p = _dot(x_ref[...], w_ref[...])
        rows = slice(base, base + n)
        q_s[rows, :] = _silu(p[:, 0:HG_DK])
        v_s[rows, :] = p[:, HG_DK:2 * HG_DK]
        row = lax.broadcasted_iota(jnp.int32, (n, HG_DK), 0) % HG_CHUNK
        for d, (k_s, g_s) in enumerate(((kf_s, gf_s), (kb_s, gb_s))):
            lbd = lb[d:d + 1, :]
            f = lbd + (1.0 - lbd) * _sigmoid(p[:, (2 + d) * HG_DK:(3 + d) * HG_DK])
            k_s[rows, :] = 1.0 - f
            g = jnp.log(f)
            step = 1
            while step < HG_CHUNK:
                if d == 0:
                    g = g + jnp.where(row >= step, pltpu.roll(g, step, 0), 0.0)
                else:
                    g = g + jnp.where(row < HG_CHUNK - step, pltpu.roll(g, n - step, 0), 0.0)
                step *= 2
            g_s[rows, :] = g
        return p[:, 4 * HG_DK:]

    project(xc_ref, 0, CTX)
    gate = project(xl_ref, CTX, SEQ)

    row = lax.broadcasted_iota(jnp.int32, (HG_CHUNK, HG_CHUNK), 0)
    colm = lax.broadcasted_iota(jnp.int32, (HG_CHUNK, HG_CHUNK), 1)
    diff = row ^ colm
    lvl = jnp.zeros_like(diff)
    for c in range(1, 7):
        lvl = lvl + (diff >= (1 << c)).astype(jnp.int32)
    lvl_f = jnp.where(row > colm, lvl, -1)
    lvl_b = jnp.where(row < colm, lvl, -1)

    def chunk(base, st, reverse, want_out):
        rows = pl.ds(base, HG_CHUNK)
        k_s, g_s = (kb_s, gb_s) if reverse else (kf_s, gf_s)
        return _hg_chunk(q_s[rows, :], k_s[rows, :], v_s[rows, :], g_s[rows, :], st,
                         lvl_b if reverse else lvl_f, reverse, want_out)

    st_f = jnp.zeros((HG_DK, HG_DK), F32)
    st_b = jnp.zeros((HG_DK, HG_DK), F32)
    for i in range(n_ctx_chunks):
        _, st_f = chunk(i * HG_CHUNK, st_f, False, False)
        _, st_b = chunk((n_ctx_chunks - 1 - i) * HG_CHUNK, st_b, True, False)

    def body(i, carry):
        st_f, st_b = carry
        bf = pl.multiple_of(i * HG_CHUNK, HG_CHUNK)
        bb = pl.multiple_of((n_lat_chunks - 1 - i) * HG_CHUNK, HG_CHUNK)
        o_f, st_f = chunk(CTX + bf, st_f, False, True)
        o_b, st_b = chunk(CTX + bb, st_b, True, True)
        of_s[pl.ds(bf, HG_CHUNK), :] = o_f
        ob_s[pl.ds(bb, HG_CHUNK), :] = o_b
        return st_f, st_b

    lax.fori_loop(0, n_lat_chunks, body, (st_f, st_b))

    o = of_s[...] + ob_s[...]
    o = o * lax.rsqrt(jnp.mean(o * o, axis=-1, keepdims=True) + EPS) * gn_ref[...]
    o_ref[...] = (o * _silu(gate)).astype(o_ref.dtype)


def _hg_core(xn_ctx, xn_lat, w_heads, lb, gnorm):
    seq_scr = pltpu.VMEM((CTX + SEQ, HG_DK), F32)
    lat_scr = pltpu.VMEM((SEQ, HG_DK), F32)
    return pl.pallas_call(
        _hg_kernel,
        grid=(BATCH, HG_HEADS),
        in_specs=[
            pl.BlockSpec((CTX, D), lambda b, h: (b, 0)),
            pl.BlockSpec((SEQ, D), lambda b, h: (b, 0)),
            pl.BlockSpec((None, D, HG_COLS), lambda b, h: (h, 0, 0)),
            pl.BlockSpec((2, HG_DK), lambda b, h: (0, h)),
            pl.BlockSpec((1, HG_DK), lambda b, h: (0, 0)),
        ],
        out_specs=pl.BlockSpec((SEQ, HG_DK), lambda b, h: (b, h)),
        out_shape=jax.ShapeDtypeStruct((BATCH * SEQ, HG_KDIM), BF16),
        scratch_shapes=[seq_scr] * 6 + [lat_scr] * 2,
        compiler_params=_params(2),
        name="hgrn2",
    )(xn_ctx, xn_lat, w_heads, lb, gnorm)


def kernel(x, c, ctx, c_ctx, w_mod, b_mod, norm_g, ffn_w13, ffn_w2, ab_w_in, ab_rpb, ab_pool_w, ab_pool_scale,
           ab_w_out, hg_w_in, hg_lb_logits, hg_gnorm, hg_w_out, final_g):
    h_lat = x.reshape(BATCH * SEQ, D)
    h_ctx = ctx.reshape(BATCH * CTX, D)

    cvec = jnp.zeros((MOD_ROWS, D), F32).at[0].set(c_ctx).at[1:1 + BATCH].set(c)
    mod = _modulation(cvec, w_mod, b_mod)

    w13 = ffn_w13.astype(BF16)
    w2 = ffn_w2.astype(BF16)

    h_lat = _ffn(h_lat, mod, norm_g, w13[0, 0], w2[0, 0], 0, 0, False)
    h_ctx = _ffn(h_ctx, mod, norm_g, w13[0, 0], w2[0, 0], 0, 0, True)
    w_in = ab_w_in[0].astype(BF16)
    ql, kl, vl, ul = _abproj(h_lat, mod, norm_g, w_in, 0, False)
    qc, kc, vc, uc = _abproj(h_ctx, mod, norm_g, w_in, 0, True)
    mix_ctx, mix_lat = _ab_core(qc, kc, vc, uc, ql, kl, vl, ul, _attn_bias_table(ab_rpb[0]),
                                ab_pool_w[0].astype(BF16), ab_pool_scale[0].reshape(1, B_WIDTH))
    w_out = ab_w_out[0].astype(BF16)
    h_lat = _outproj(mix_lat, h_lat, mod, w_out, 0, False)
    h_ctx = _outproj(mix_ctx, h_ctx, mod, w_out, 0, True)
    h_lat = _ffn(h_lat, mod, norm_g, w13[0, 1], w2[0, 1], 0, 2, False)
    h_ctx = _ffn(h_ctx, mod, norm_g, w13[0, 1], w2[0, 1], 0, 2, True)

    h_lat = _ffn(h_lat, mod, norm_g, w13[1, 0], w2[1, 0], 1, 0, False)
    h_ctx = _ffn(h_ctx, mod, norm_g, w13[1, 0], w2[1, 0], 1, 0, True)
    xn_lat = _adaln_rows(h_lat, mod, norm_g, 1, False)
    xn_ctx = _adaln_rows(h_ctx, mod, norm_g, 1, True)
    sm = jax.nn.softmax(hg_lb_logits.astype(F32), axis=0)
    lb = (jnp.cumsum(sm, axis=0) - sm[0:1])[1]
    w_heads = (hg_w_in[0].reshape(D, 5, HG_HEADS, HG_DK).transpose(2, 0, 1, 3)
               .reshape(HG_HEADS, D, HG_COLS).astype(BF16))
    o = _hg_core(xn_ctx, xn_lat, w_heads, lb, hg_gnorm[0].reshape(1, HG_DK))
    h_lat = _outproj(o, h_lat, mod, hg_w_out[0].astype(BF16), 1, False)
    out = _ffn(h_lat, mod, norm_g, w13[1, 1], w2[1, 1], 1, 2, False, final_g=final_g)
    return out.reshape(BATCH, SEQ, D)
```

```python
import functools

import jax
import jax.numpy as jnp
from jax import lax
from jax.experimental import pallas as pl
from jax.experimental.pallas import tpu as pltpu

D = 1024
BATCH = 16
SEQ = 2048
DEPTH = 2
GRID_W = 64
CTX = 256
EPS = 1e-6
N_MOD = 9
D_FF = 2816
A_HEADS = 8
A_DH = 64
A_WIDTH = 512
WIN_R = 8
WIN_C = 16
POOL_WINDOWS = (2, 4, 8, 16)
B_WIDTH = 512
B_GROUP = 128
HG_HEADS = 8
HG_DK = 128
HG_KDIM = 1024

ROWS = SEQ // GRID_W
MOD_ROWS = 24
ROW_TILE = 512
FF_CHUNK = 256
HG_CHUNK = 128
HG_COLS = 5 * HG_DK
NEG_BIG = -1e30
VMEM_LIMIT = 56 * 1024 * 1024

F32 = jnp.float32
BF16 = jnp.bfloat16


def _dot(a, b):
    return jnp.dot(a, b, preferred_element_type=F32)


def _dot_nt(a, b):
    return lax.dot_general(a, b, (((1,), (1,)), ((), ())), preferred_element_type=F32)


def _dot_tn(a, b):
    return lax.dot_general(a, b, (((0,), (0,)), ((), ())), preferred_element_type=F32)


def _sigmoid(x):
    return 1.0 / (1.0 + jnp.exp(-x))


def _silu(x):
    return x * _sigmoid(x)


def _params(n_axes):
    return pltpu.CompilerParams(dimension_semantics=("arbitrary",) * n_axes, vmem_limit_bytes=VMEM_LIMIT)


def _const_spec(shape):
    nd = len(shape)
    return pl.BlockSpec(shape, lambda *_: (0,) * nd, pipeline_mode=pl.Buffered(1))


def _adaln(x, g, shift, scale):
    ms = jnp.mean(x * x, axis=-1, keepdims=True)
    return (x * lax.rsqrt(ms + EPS)) * g * (1.0 + scale) + shift


def _mod_kernel(c_ref, w_ref, b_ref, o_ref):
    s = _silu(c_ref[...]).astype(BF16)
    o_ref[...] = _dot(s, w_ref[...].astype(BF16)) + b_ref[...]


def _modulation(cvec, w_mod, b_mod):
    nblk = N_MOD
    out = pl.pallas_call(
        _mod_kernel,
        grid=(DEPTH, nblk),
        in_specs=[
            pl.BlockSpec((MOD_ROWS, D), lambda l, j: (0, 0)),
            pl.BlockSpec((None, D, D), lambda l, j: (l, 0, j)),
            pl.BlockSpec((None, 1, D), lambda l, j: (l, 0, j)),
        ],
        out_specs=pl.BlockSpec((None, MOD_ROWS, D), lambda l, j: (l, 0, j)),
        out_shape=jax.ShapeDtypeStruct((DEPTH, MOD_ROWS, N_MOD * D), F32),
        compiler_params=_params(2),
        name="modulation",
    )(cvec, w_mod, b_mod.reshape(DEPTH, 1, N_MOD * D))
    return out.reshape(DEPTH, MOD_ROWS, N_MOD, D)


def _mod_spec(layer, is_ctx):
    tiles_per_batch = SEQ // ROW_TILE
    if is_ctx:
        return pl.BlockSpec((None, None, N_MOD, D), lambda t: (layer, 0, 0, 0))
    return pl.BlockSpec((None, None, N_MOD, D), lambda t: (layer, 1 + t // tiles_per_batch, 0, 0))


def _norm_spec(layer):
    return pl.BlockSpec((None, 3, D), lambda t: (layer, 0, 0))


def _row_spec(width):
    return pl.BlockSpec((ROW_TILE, width), lambda t: (t, 0))


def _rows_kernel(sub, pre_mix, post, *refs):
    refs = list(refs)
    h_ref, mod_ref, g_ref = refs[:3]
    del refs[:3]
    if pre_mix:
        x_ref, wmix_ref = refs[:2]
        del refs[:2]
    w13_ref, w2_ref = refs[:2]
    del refs[:2]
    if post in ("ab", "final"):
        wpost_ref = refs.pop(0)
    o_ref = refs.pop(0)

    h = h_ref[...]
    if pre_mix:
        h = h + mod_ref[5:6, :] * _dot(x_ref[...], wmix_ref[...])
    shift = mod_ref[3 * sub:3 * sub + 1, :]
    scale = mod_ref[3 * sub + 1:3 * sub + 2, :]
    gate = mod_ref[3 * sub + 2:3 * sub + 3, :]
    xb = _adaln(h, g_ref[sub:sub + 1, :], shift, scale).astype(BF16)
    acc = jnp.zeros((ROW_TILE, D), F32)
    for c in range(D_FF // FF_CHUNK):
        lo = c * FF_CHUNK
        a = _dot(xb, w13_ref[:, lo:lo + FF_CHUNK])
        b = _dot(xb, w13_ref[:, D_FF + lo:D_FF + lo + FF_CHUNK])
        act = (_silu(a) * b).astype(BF16)
        acc = acc + _dot(act, w2_ref[lo:lo + FF_CHUNK, :])
    y = h + (0.5 * gate) * acc
    if post == "final":
        ms = jnp.mean(y * y, axis=-1, keepdims=True)
        y = (y * lax.rsqrt(ms + EPS)) * wpost_ref[...]
    o_ref[...] = y
    if post in ("ab", "xn"):
        xm = _adaln(y, g_ref[1:2, :], mod_ref[3:4, :], mod_ref[4:5, :]).astype(BF16)
    if post == "xn":
        refs[0][...] = xm
    if post == "ab":
        q_ref, k_ref, v_ref, u_ref = refs
        q_ref[...] = (_dot(xm, wpost_ref[:, 0:A_WIDTH]) * (A_DH ** -0.5)).astype(BF16)
        k_ref[...] = _dot(xm, wpost_ref[:, A_WIDTH:2 * A_WIDTH]).astype(BF16)
        v_ref[...] = _dot(xm, wpost_ref[:, 2 * A_WIDTH:3 * A_WIDTH]).astype(BF16)
        u_ref[...] = _dot(xm, wpost_ref[:, 3 * A_WIDTH:])


def _rows(h, mod, norm_g, w13, w2, layer, sub, is_ctx, mix=None, post=None, w_post=None):
    n = h.shape[0]
    which = sub // 2
    in_specs = [_row_spec(D), _mod_spec(layer, is_ctx), _norm_spec(layer)]
    args = [h, mod, norm_g]
    if mix is not None:
        in_specs += [_row_spec(D), _const_spec((D, D))]
        args += list(mix)
    in_specs += [
        pl.BlockSpec((None, None, D, 2 * D_FF), lambda t: (layer, which, 0, 0), pipeline_mode=pl.Buffered(1)),
        pl.BlockSpec((None, None, D_FF, D), lambda t: (layer, which, 0, 0), pipeline_mode=pl.Buffered(1)),
    ]
    args += [w13, w2]
    out_specs = [_row_spec(D)]
    out_shape = [jax.ShapeDtypeStruct((n, D), F32)]
    if post == "final":
        in_specs.append(_const_spec((1, D)))
        args.append(w_post.reshape(1, D))
    elif post == "ab":
        in_specs.append(_const_spec((D, 3 * A_WIDTH + B_WIDTH)))
        args.append(w_post)
        out_specs += [_row_spec(A_WIDTH)] * 3 + [_row_spec(B_WIDTH)]
        out_shape += [jax.ShapeDtypeStruct((n, A_WIDTH), BF16)] * 3 + [jax.ShapeDtypeStruct((n, B_WIDTH), F32)]
    elif post == "xn":
        out_specs.append(_row_spec(D))
        out_shape.append(jax.ShapeDtypeStruct((n, D), BF16))
    return pl.pallas_call(
        functools.partial(_rows_kernel, sub, mix is not None, post),
        grid=(n // ROW_TILE,),
        in_specs=in_specs,
        out_specs=out_specs,
        out_shape=out_shape,
        compiler_params=_params(1),
        name="rows",
    )(*args)


def _attn_bias_table(rpb):
    col = jnp.arange(GRID_W)
    win_c0 = jnp.clip(col - WIN_C // 2, 0, GRID_W - WIN_C)
    kc = jnp.arange(GRID_W)
    col_ok = (kc[None, :] >= win_c0[:, None]) & (kc[None, :] < win_c0[:, None] + WIN_C)
    rel_c = jnp.clip(kc[None, :] - col[:, None], -(WIN_C - 1), WIN_C - 1) + (WIN_C - 1)
    by_col = jnp.where(col_ok[None, None], rpb.astype(F32)[:, :, rel_c], NEG_BIG)
    tab = jnp.stack([by_col[:, WIN_R - 1 - dr:2 * WIN_R - 1 - dr] for dr in range(WIN_R)])
    tab = tab.transpose(0, 1, 3, 2, 4)
    return tab.reshape(WIN_R, A_HEADS * GRID_W, WIN_R * GRID_W)


def _softmax_parts(s_parts):
    m = s_parts[0].max(axis=-1, keepdims=True)
    for s in s_parts[1:]:
        m = jnp.maximum(m, s.max(axis=-1, keepdims=True))
    ps = [jnp.exp(s - m) for s in s_parts]
    den = ps[0].sum(axis=-1, keepdims=True)
    for p in ps[1:]:
        den = den + p.sum(axis=-1, keepdims=True)
    return [p.astype(BF16) for p in ps], 1.0 / den


def _stack_heads(x2, first):
    zero = jnp.zeros((), x2.dtype)
    return jnp.concatenate([jnp.where(first, x2, zero), jnp.where(first, zero, x2)], axis=0)


def _unstack_heads(o, first):
    m = o.shape[0] // 2
    return jnp.where(first, o[:m], o[m:])


def _shift_rows(x, k, row, n):
    if k > 0:
        return jnp.where(row >= k, pltpu.roll(x, k, 0), 0.0)
    return jnp.where(row < n + k, pltpu.roll(x, n + k, 0), 0.0)


def _pool_mix(u_ref, pw_ref, ps_ref, o_ref, n):
    row = lax.broadcasted_iota(jnp.int32, (n, B_GROUP), 0)
    rowf = row.astype(F32)
    for gi, w in enumerate(POOL_WINDOWS):
        x = u_ref[:, gi * B_GROUP:(gi + 1) * B_GROUP]
        half = w // 2
        back = x
        fwd = x
        m = 1
        while m < half:
            back = back + _shift_rows(back, m, row, n)
            fwd = fwd + _shift_rows(fwd, -m, row, n)
            m *= 2
        total = _shift_rows(back, 1, row, n) + fwd
        cnt = jnp.minimum(rowf + half, float(n)) - jnp.maximum(rowf - half, 0.0)
        d = (total / cnt - x).astype(BF16)
        y = _dot(d, pw_ref[gi]) * ps_ref[:, gi * B_GROUP:(gi + 1) * B_GROUP]
        o_ref[:, A_WIDTH + gi * B_GROUP:A_WIDTH + (gi + 1) * B_GROUP] = y.astype(o_ref.dtype)


def _ab_core_kernel(qc_ref, kc_ref, vc_ref, uc_ref, ql_ref, kl_ref, vl_ref, ul_ref, bias_ref, pw_ref, ps_ref,
                    oc_ref, ol_ref):
    lane = lax.broadcasted_iota(jnp.int32, (1, 2 * A_DH), 1)
    first = lane < A_DH
    pairs = [slice(p * 2 * A_DH, (p + 1) * 2 * A_DH) for p in range(A_HEADS // 2)]

    qs = [_stack_heads(qc_ref[:, cols], first) for cols in pairs]
    s_all = jnp.concatenate([_dot_nt(q, kc_ref[:, cols]) for q, cols in zip(qs, pairs)], axis=0)
    (p_all,), inv = _softmax_parts([s_all])
    for i, cols in enumerate(pairs):
        rows = slice(i * 2 * CTX, (i + 1) * 2 * CTX)
        o = _dot(p_all[rows], vc_ref[:, cols]) * inv[rows]
        oc_ref[:, cols] = _unstack_heads(o, first).astype(oc_ref.dtype)

    def row_body(r, carry):
        r0 = jnp.clip(r - WIN_R // 2, 0, ROWS - WIN_R)
        qrows = pl.ds(pl.multiple_of(r * GRID_W, GRID_W), GRID_W)
        krows = pl.ds(pl.multiple_of(r0 * GRID_W, GRID_W), WIN_R * GRID_W)
        qs = [_stack_heads(ql_ref[qrows, cols], first) for cols in pairs]
        s_loc = jnp.concatenate([_dot_nt(q, kl_ref[krows, cols]) for q, cols in zip(qs, pairs)], axis=0)
        s_ctx = jnp.concatenate([_dot_nt(q, kc_ref[:, cols]) for q, cols in zip(qs, pairs)], axis=0)
        (p_loc, p_ctx), inv = _softmax_parts([s_loc + bias_ref[r - r0], s_ctx])
        for i, cols in enumerate(pairs):
            rows = slice(i * 2 * GRID_W, (i + 1) * 2 * GRID_W)
            o = (_dot(p_loc[rows], vl_ref[krows, cols]) + _dot(p_ctx[rows], vc_ref[:, cols])) * inv[rows]
            ol_ref[qrows, cols] = _unstack_heads(o, first).astype(ol_ref.dtype)
        return carry

    lax.fori_loop(0, ROWS, row_body, 0)

    _pool_mix(uc_ref, pw_ref, ps_ref, oc_ref, CTX)
    _pool_mix(ul_ref, pw_ref, ps_ref, ol_ref, SEQ)


def _ab_core(qc, kc, vc, uc, ql, kl, vl, ul, bias, pool_w, pool_scale):
    cspec = lambda w: pl.BlockSpec((CTX, w), lambda b: (b, 0))
    lspec = lambda w: pl.BlockSpec((SEQ, w), lambda b: (b, 0))
    return pl.pallas_call(
        _ab_core_kernel,
        grid=(BATCH,),
        in_specs=[
            cspec(A_WIDTH), cspec(A_WIDTH), cspec(A_WIDTH), cspec(B_WIDTH),
            lspec(A_WIDTH), lspec(A_WIDTH), lspec(A_WIDTH), lspec(B_WIDTH),
            _const_spec((WIN_R, A_HEADS * GRID_W, WIN_R * GRID_W)),
            _const_spec((len(POOL_WINDOWS), B_GROUP, B_GROUP)),
            _const_spec((1, B_WIDTH)),
        ],
        out_specs=[cspec(D), lspec(D)],
        out_shape=[
            jax.ShapeDtypeStruct((BATCH * CTX, D), BF16),
            jax.ShapeDtypeStruct((BATCH * SEQ, D), BF16),
        ],
        compiler_params=_params(1),
        name="ab_core",
    )(qc, kc, vc, uc, ql, kl, vl, ul, bias, pool_w, pool_scale)


HG_LEVELS = (2, 4, 8, 16, 32, 64)
LOG2E = 1.4426950408889634


def _hg_constants():
    t = jnp.arange(HG_CHUNK)
    low = (t[:, None] >= t[None, :]).astype(BF16)
    cum = jnp.stack([jnp.concatenate([low, low], axis=1), jnp.concatenate([low.T, low.T], axis=1)])
    upper = jnp.stack([(t & c) != 0 for c in HG_LEVELS])
    sgn_f = jnp.where(upper, LOG2E, -LOG2E).astype(F32)
    sgn = jnp.stack([sgn_f, -sgn_f])[:, :, :, None]
    return cum, jnp.broadcast_to(sgn, (2, len(HG_LEVELS), HG_CHUNK, HG_DK))


def _bcast_row(x3, k):
    return jnp.broadcast_to(x3[:, k:k + 1, :], x3.shape)


def _level_refs(g, reverse):
    ngrp = HG_CHUNK // 8
    g3 = g.reshape(ngrp, 8, HG_DK)
    sub = lax.broadcasted_iota(jnp.int32, g3.shape, 1)
    pick = (lambda k: k + 1) if reverse else (lambda k: k)
    refs = [jnp.where(sub < 4, _bcast_row(g3, pick(1)), _bcast_row(g3, pick(5))), _bcast_row(g3, pick(3))]
    edge = _bcast_row(g3, 0 if reverse else 7)
    for c in HG_LEVELS[2:]:
        span = 2 * c // 8
        parts = []
        for blk in range(ngrp // span):
            src = blk * span + span // 2 - (0 if reverse else 1)
            parts.extend([edge[src:src + 1]] * span)
        refs.append(jnp.concatenate(parts, axis=0))
    return [x.reshape(HG_CHUNK, HG_DK) for x in refs]


def _hg_scores(q, k, g, sgn_ref, level_of, reverse):
    row = lax.broadcasted_iota(jnp.int32, (HG_CHUNK, HG_CHUNK), 0)
    colm = lax.broadcasted_iota(jnp.int32, (HG_CHUNK, HG_CHUNK), 1)
    a = jnp.where(row == colm, jnp.sum(q * k, axis=-1, keepdims=True), 0.0)
    qb = q.astype(BF16)
    kb = k.astype(BF16)
    a = jnp.where(level_of == 0, _dot_nt(qb * (1.0 - k).astype(BF16), kb), a)
    for i, ref in enumerate(_level_refs(g, reverse)):
        w = jnp.exp2((g - ref) * sgn_ref[i]).astype(BF16)
        a = jnp.where(level_of == i + 1, _dot_nt(qb * w, kb * w), a)
    return a.astype(BF16)


def _hg_state(k, v, g, st, reverse):
    g_end = g[0:1, :] if reverse else g[HG_CHUNK - 1:HG_CHUNK, :]
    kp = (k * jnp.exp(g_end - g)).astype(BF16)
    return st * jnp.exp(g_end) + _dot_tn(v.astype(BF16), kp)


def _hg_out(q, v, g, st, a):
    qp = (q * jnp.exp(g)).astype(BF16)
    return _dot_nt(qp, st.astype(BF16)) + _dot(a, v.astype(BF16))


def _hg_kernel(xc_ref, xl_ref, w_ref, lb_ref, gn_ref, cum_ref, sgn_ref, o_ref,
               q_s, v_s, kf_s, kb_s, gf_s, gb_s, of_s, ob_s, a_s):
    n_ctx_chunks = CTX // HG_CHUNK
    n_lat_chunks = SEQ // HG_CHUNK
    lb = lb_ref[...]

    def project(x_ref, base, n):
        p = _dot(x_ref[...], w_ref[...])
        rows = slice(base, base + n)
        q_s[rows, :] = _silu(p[:, 0:HG_DK])
        v_s[rows, :] = p[:, HG_DK:2 * HG_DK]
        for d, (k_s, g_s) in enumerate(((kf_s, gf_s), (kb_s, gb_s))):
            lbd = lb[d:d + 1, :]
            f = lbd + (1.0 - lbd) * _sigmoid(p[:, (2 + d) * HG_DK:(3 + d) * HG_DK])
            k_s[rows, :] = 1.0 - f
            lf = jnp.log(f)
            hi = lf.astype(BF16)
            lo = (lf - hi.astype(F32)).astype(BF16)
            for i in range(n // HG_CHUNK):
                blk = slice(i * HG_CHUNK, (i + 1) * HG_CHUNK)
                g_s[base + i * HG_CHUNK:base + (i + 1) * HG_CHUNK, :] = _dot(
                    cum_ref[d], jnp.concatenate([hi[blk], lo[blk]], axis=0))
        return p[:, 4 * HG_DK:]

    project(xc_ref, 0, CTX)
    gate = project(xl_ref, CTX, SEQ)

    row = lax.broadcasted_iota(jnp.int32, (HG_CHUNK, HG_CHUNK), 0)
    colm = lax.broadcasted_iota(jnp.int32, (HG_CHUNK, HG_CHUNK), 1)
    diff = row ^ colm
    lvl = jnp.zeros_like(diff)
    for c in range(1, 7):
        lvl = lvl + (diff >= (1 << c)).astype(jnp.int32)
    lvl_f = jnp.where(row > colm, lvl, -1)
    lvl_b = jnp.where(row < colm, lvl, -1)

    def operands(base, reverse):
        rows = pl.ds(base, HG_CHUNK)
        k_s, g_s = (kb_s, gb_s) if reverse else (kf_s, gf_s)
        return q_s[rows, :], k_s[rows, :], v_s[rows, :], g_s[rows, :]

    st_f = jnp.zeros((HG_DK, HG_DK), F32)
    st_b = jnp.zeros((HG_DK, HG_DK), F32)
    for i in range(n_ctx_chunks):
        _, k, v, g = operands(i * HG_CHUNK, False)
        st_f = _hg_state(k, v, g, st_f, False)
        _, k, v, g = operands((n_ctx_chunks - 1 - i) * HG_CHUNK, True)
        st_b = _hg_state(k, v, g, st_b, True)

    def lat_base(i, reverse):
        j = (n_lat_chunks - 1 - i) if reverse else i
        return pl.multiple_of(j * HG_CHUNK, HG_CHUNK)

    def scores(i):
        for d, reverse in enumerate((False, True)):
            q, k, _, g = operands(CTX + lat_base(i, reverse), reverse)
            a_s[d, i] = _hg_scores(q, k, g, sgn_ref.at[d], lvl_b if reverse else lvl_f, reverse)

    def body(i, carry):
        sts = list(carry)
        for d, (reverse, o_s) in enumerate(((False, of_s), (True, ob_s))):
            base = lat_base(i, reverse)
            q, k, v, g = operands(CTX + base, reverse)
            o_s[pl.ds(base, HG_CHUNK), :] = _hg_out(q, v, g, sts[d], a_s[d, i])
            sts[d] = _hg_state(k, v, g, sts[d], reverse)
        scores(jnp.minimum(i + 1, n_lat_chunks - 1))
        return tuple(sts)

    scores(0)
    lax.fori_loop(0, n_lat_chunks, body, (st_f, st_b), unroll=2)

    o = of_s[...] + ob_s[...]
    o = o * lax.rsqrt(jnp.mean(o * o, axis=-1, keepdims=True) + EPS) * gn_ref[...]
    o_ref[...] = (o * _silu(gate)).astype(o_ref.dtype)


def _hg_core(xn_ctx, xn_lat, w_heads, lb, gnorm):
    seq_scr = pltpu.VMEM((CTX + SEQ, HG_DK), F32)
    lat_scr = pltpu.VMEM((SEQ, HG_DK), F32)
    a_scr = pltpu.VMEM((2, SEQ // HG_CHUNK, HG_CHUNK, HG_CHUNK), BF16)
    cum, sgn = _hg_constants()
    return pl.pallas_call(
        _hg_kernel,
        grid=(BATCH, HG_HEADS),
        in_specs=[
            pl.BlockSpec((CTX, D), lambda b, h: (b, 0)),
            pl.BlockSpec((SEQ, D), lambda b, h: (b, 0)),
            pl.BlockSpec((None, D, HG_COLS), lambda b, h: (h, 0, 0)),
            pl.BlockSpec((2, HG_DK), lambda b, h: (0, h)),
            pl.BlockSpec((1, HG_DK), lambda b, h: (0, 0)),
            _const_spec(cum.shape),
            _const_spec(sgn.shape),
        ],
        out_specs=pl.BlockSpec((SEQ, HG_DK), lambda b, h: (b, h)),
        out_shape=jax.ShapeDtypeStruct((BATCH * SEQ, HG_KDIM), BF16),
        scratch_shapes=[seq_scr] * 6 + [lat_scr] * 2 + [a_scr],
        compiler_params=_params(2),
        name="hgrn2",
    )(xn_ctx, xn_lat, w_heads, lb, gnorm, cum, sgn)


def kernel(x, c, ctx, c_ctx, w_mod, b_mod, norm_g, ffn_w13, ffn_w2, ab_w_in, ab_rpb, ab_pool_w, ab_pool_scale,
           ab_w_out, hg_w_in, hg_lb_logits, hg_gnorm, hg_w_out, final_g):
    h_lat = x.reshape(BATCH * SEQ, D)
    h_ctx = ctx.reshape(BATCH * CTX, D)

    cvec = jnp.zeros((MOD_ROWS, D), F32).at[0].set(c_ctx).at[1:1 + BATCH].set(c)
    mod = _modulation(cvec, w_mod, b_mod)

    w13 = ffn_w13.astype(BF16)
    w2 = ffn_w2.astype(BF16)
    ffn = functools.partial(_rows, mod=mod, norm_g=norm_g, w13=w13, w2=w2)

    w_in = ab_w_in[0].astype(BF16)
    h_lat, ql, kl, vl, ul = ffn(h_lat, layer=0, sub=0, is_ctx=False, post="ab", w_post=w_in)
    h_ctx, qc, kc, vc, uc = ffn(h_ctx, layer=0, sub=0, is_ctx=True, post="ab", w_post=w_in)
    mix_ctx, mix_lat = _ab_core(qc, kc, vc, uc, ql, kl, vl, ul, _attn_bias_table(ab_rpb[0]),
                                ab_pool_w[0].astype(BF16), ab_pool_scale[0].reshape(1, B_WIDTH))
    w_out = ab_w_out[0].astype(BF16)
    (h_lat,) = ffn(h_lat, layer=0, sub=2, is_ctx=False, mix=(mix_lat, w_out))
    (h_ctx,) = ffn(h_ctx, layer=0, sub=2, is_ctx=True, mix=(mix_ctx, w_out))

    h_lat, xn_lat = ffn(h_lat, layer=1, sub=0, is_ctx=False, post="xn")
    _, xn_ctx = ffn(h_ctx, layer=1, sub=0, is_ctx=True, post="xn")
    sm = jax.nn.softmax(hg_lb_logits.astype(F32), axis=0)
    lb = (jnp.cumsum(sm, axis=0) - sm[0:1])[1]
    w_heads = (hg_w_in[0].reshape(D, 5, HG_HEADS, HG_DK).transpose(2, 0, 1, 3)
               .reshape(HG_HEADS, D, HG_COLS).astype(BF16))
    o = _hg_core(xn_ctx, xn_lat, w_heads, lb, hg_gnorm[0].reshape(1, HG_DK))
    (out,) = ffn(h_lat, layer=1, sub=2, is_ctx=False, mix=(o, hg_w_out[0].astype(BF16)), post="final",
                 w_post=final_g)
    return out.reshape(BATCH, SEQ, D)
```

```python
import functools

import jax
import jax.numpy as jnp
from jax import lax
from jax.experimental import pallas as pl
from jax.experimental.pallas import tpu as pltpu

D = 1024
BATCH = 16
SEQ = 2048
DEPTH = 2
GRID_W = 64
CTX = 256
EPS = 1e-6
N_MOD = 9
D_FF = 2816
A_HEADS = 8
A_DH = 64
A_WIDTH = 512
WIN_R = 8
WIN_C = 16
POOL_WINDOWS = (2, 4, 8, 16)
B_WIDTH = 512
B_GROUP = 128
HG_HEADS = 8
HG_DK = 128
HG_KDIM = 1024

ROWS = SEQ // GRID_W
MOD_ROWS = 24
ROW_TILE = 512
FF_CHUNK = 256
HG_CHUNK = 128
HG_COLS = 5 * HG_DK
HG_STEP_HEADS = 2
NEG_BIG = -1e30
VMEM_LIMIT = 56 * 1024 * 1024

F32 = jnp.float32
BF16 = jnp.bfloat16


def _dot(a, b):
    return jnp.dot(a, b, preferred_element_type=F32)


def _dot_nt(a, b):
    return lax.dot_general(a, b, (((1,), (1,)), ((), ())), preferred_element_type=F32)


def _dot_tn(a, b):
    return lax.dot_general(a, b, (((0,), (0,)), ((), ())), preferred_element_type=F32)


def _sigmoid(x):
    return 1.0 / (1.0 + jnp.exp(-x))


def _silu(x):
    return x * _sigmoid(x)


def _params(n_axes):
    return pltpu.CompilerParams(dimension_semantics=("arbitrary",) * n_axes, vmem_limit_bytes=VMEM_LIMIT)


def _const_spec(shape):
    nd = len(shape)
    return pl.BlockSpec(shape, lambda *_: (0,) * nd, pipeline_mode=pl.Buffered(1))


def _adaln(x, g, shift, scale):
    ms = jnp.mean(x * x, axis=-1, keepdims=True)
    return (x * lax.rsqrt(ms + EPS)) * g * (1.0 + scale) + shift


def _mod_kernel(c_ref, w_ref, b_ref, o_ref):
    s = _silu(c_ref[...]).astype(BF16)
    o_ref[...] = _dot(s, w_ref[...].astype(BF16)) + b_ref[...]


def _modulation(cvec, w_mod, b_mod):
    nblk = N_MOD
    out = pl.pallas_call(
        _mod_kernel,
        grid=(DEPTH, nblk),
        in_specs=[
            pl.BlockSpec((MOD_ROWS, D), lambda l, j: (0, 0)),
            pl.BlockSpec((None, D, D), lambda l, j: (l, 0, j)),
            pl.BlockSpec((None, 1, D), lambda l, j: (l, 0, j)),
        ],
        out_specs=pl.BlockSpec((None, MOD_ROWS, D), lambda l, j: (l, 0, j)),
        out_shape=jax.ShapeDtypeStruct((DEPTH, MOD_ROWS, N_MOD * D), F32),
        compiler_params=_params(2),
        name="modulation",
    )(cvec, w_mod, b_mod.reshape(DEPTH, 1, N_MOD * D))
    return out.reshape(DEPTH, MOD_ROWS, N_MOD, D)


def _mod_spec(layer, is_ctx):
    tiles_per_batch = SEQ // ROW_TILE
    if is_ctx:
        return pl.BlockSpec((None, None, N_MOD, D), lambda t: (layer, 0, 0, 0))
    return pl.BlockSpec((None, None, N_MOD, D), lambda t: (layer, 1 + t // tiles_per_batch, 0, 0))


def _norm_spec(layer):
    return pl.BlockSpec((None, 3, D), lambda t: (layer, 0, 0))


def _row_spec(width):
    return pl.BlockSpec((ROW_TILE, width), lambda t: (t, 0))


def _rows_kernel(sub, pre_mix, post, *refs):
    refs = list(refs)
    h_ref, mod_ref, g_ref = refs[:3]
    del refs[:3]
    if pre_mix:
        x_ref, wmix_ref = refs[:2]
        del refs[:2]
    w13_ref, w2_ref = refs[:2]
    del refs[:2]
    if post in ("ab", "final"):
        wpost_ref = refs.pop(0)
    o_ref = refs.pop(0)

    h = h_ref[...]
    if pre_mix:
        h = h + mod_ref[5:6, :] * _dot(x_ref[...], wmix_ref[...])
    shift = mod_ref[3 * sub:3 * sub + 1, :]
    scale = mod_ref[3 * sub + 1:3 * sub + 2, :]
    gate = mod_ref[3 * sub + 2:3 * sub + 3, :]
    xb = _adaln(h, g_ref[sub:sub + 1, :], shift, scale).astype(BF16)
    acc = jnp.zeros((ROW_TILE, D), F32)
    for c in range(D_FF // FF_CHUNK):
        lo = c * FF_CHUNK
        a = _dot(xb, w13_ref[:, lo:lo + FF_CHUNK])
        b = _dot(xb, w13_ref[:, D_FF + lo:D_FF + lo + FF_CHUNK])
        act = (_silu(a) * b).astype(BF16)
        acc = acc + _dot(act, w2_ref[lo:lo + FF_CHUNK, :])
    y = h + (0.5 * gate) * acc
    if post == "final":
        ms = jnp.mean(y * y, axis=-1, keepdims=True)
        y = (y * lax.rsqrt(ms + EPS)) * wpost_ref[...]
    o_ref[...] = y
    if post in ("ab", "xn"):
        xm = _adaln(y, g_ref[1:2, :], mod_ref[3:4, :], mod_ref[4:5, :]).astype(BF16)
    if post == "xn":
        refs[0][...] = xm
    if post == "ab":
        q_ref, k_ref, v_ref, u_ref = refs
        q_ref[...] = (_dot(xm, wpost_ref[:, 0:A_WIDTH]) * (A_DH ** -0.5)).astype(BF16)
        k_ref[...] = _dot(xm, wpost_ref[:, A_WIDTH:2 * A_WIDTH]).astype(BF16)
        v_ref[...] = _dot(xm, wpost_ref[:, 2 * A_WIDTH:3 * A_WIDTH]).astype(BF16)
        u_ref[...] = _dot(xm, wpost_ref[:, 3 * A_WIDTH:])


def _rows(h, mod, norm_g, w13, w2, layer, sub, is_ctx, mix=None, post=None, w_post=None):
    n = h.shape[0]
    which = sub // 2
    in_specs = [_row_spec(D), _mod_spec(layer, is_ctx), _norm_spec(layer)]
    args = [h, mod, norm_g]
    if mix is not None:
        in_specs += [_row_spec(D), _const_spec((D, D))]
        args += list(mix)
    in_specs += [
        pl.BlockSpec((None, None, D, 2 * D_FF), lambda t: (layer, which, 0, 0), pipeline_mode=pl.Buffered(1)),
        pl.BlockSpec((None, None, D_FF, D), lambda t: (layer, which, 0, 0), pipeline_mode=pl.Buffered(1)),
    ]
    args += [w13, w2]
    out_specs = [_row_spec(D)]
    out_shape = [jax.ShapeDtypeStruct((n, D), F32)]
    if post == "final":
        in_specs.append(_const_spec((1, D)))
        args.append(w_post.reshape(1, D))
    elif post == "ab":
        in_specs.append(_const_spec((D, 3 * A_WIDTH + B_WIDTH)))
        args.append(w_post)
        out_specs += [_row_spec(A_WIDTH)] * 3 + [_row_spec(B_WIDTH)]
        out_shape += [jax.ShapeDtypeStruct((n, A_WIDTH), BF16)] * 3 + [jax.ShapeDtypeStruct((n, B_WIDTH), F32)]
    elif post == "xn":
        out_specs.append(_row_spec(D))
        out_shape.append(jax.ShapeDtypeStruct((n, D), BF16))
    return pl.pallas_call(
        functools.partial(_rows_kernel, sub, mix is not None, post),
        grid=(n // ROW_TILE,),
        in_specs=in_specs,
        out_specs=out_specs,
        out_shape=out_shape,
        compiler_params=_params(1),
        name="rows",
    )(*args)


def _attn_bias_table(rpb):
    col = jnp.arange(GRID_W)
    win_c0 = jnp.clip(col - WIN_C // 2, 0, GRID_W - WIN_C)
    kc = jnp.arange(GRID_W)
    col_ok = (kc[None, :] >= win_c0[:, None]) & (kc[None, :] < win_c0[:, None] + WIN_C)
    rel_c = jnp.clip(kc[None, :] - col[:, None], -(WIN_C - 1), WIN_C - 1) + (WIN_C - 1)
    by_col = jnp.where(col_ok[None, :, None, :], rpb.astype(F32)[:, :, rel_c].transpose(0, 2, 1, 3), NEG_BIG)
    tab = jnp.stack([by_col[:, :, WIN_R - 1 - dr:2 * WIN_R - 1 - dr] for dr in range(WIN_R)])
    return tab.reshape(WIN_R, A_HEADS * GRID_W, WIN_R * GRID_W)


def _softmax_parts(s_parts):
    m = s_parts[0].max(axis=-1, keepdims=True)
    for s in s_parts[1:]:
        m = jnp.maximum(m, s.max(axis=-1, keepdims=True))
    ps = [jnp.exp(s - m) for s in s_parts]
    den = ps[0].sum(axis=-1, keepdims=True)
    for p in ps[1:]:
        den = den + p.sum(axis=-1, keepdims=True)
    return [p.astype(BF16) for p in ps], 1.0 / den


def _stack_heads(x2, first):
    zero = jnp.zeros((), x2.dtype)
    return jnp.concatenate([jnp.where(first, x2, zero), jnp.where(first, zero, x2)], axis=0)


def _unstack_heads(o, first):
    m = o.shape[0] // 2
    return jnp.where(first, o[:m], o[m:])


def _shift_rows(x, k, row, n):
    if k > 0:
        return jnp.where(row >= k, pltpu.roll(x, k, 0), 0.0)
    return jnp.where(row < n + k, pltpu.roll(x, n + k, 0), 0.0)


def _pool_mix(u_ref, pw_ref, ps_ref, o_ref, n):
    row = lax.broadcasted_iota(jnp.int32, (n, B_GROUP), 0)
    rowf = row.astype(F32)
    for gi, w in enumerate(POOL_WINDOWS):
        x = u_ref[:, gi * B_GROUP:(gi + 1) * B_GROUP]
        half = w // 2
        back = x
        fwd = x
        m = 1
        while m < half:
            back = back + _shift_rows(back, m, row, n)
            fwd = fwd + _shift_rows(fwd, -m, row, n)
            m *= 2
        total = _shift_rows(back, 1, row, n) + fwd
        cnt = jnp.minimum(rowf + half, float(n)) - jnp.maximum(rowf - half, 0.0)
        d = (total / cnt - x).astype(BF16)
        y = _dot(d, pw_ref[gi]) * ps_ref[:, gi * B_GROUP:(gi + 1) * B_GROUP]
        o_ref[:, A_WIDTH + gi * B_GROUP:A_WIDTH + (gi + 1) * B_GROUP] = y.astype(o_ref.dtype)


def _ab_core_kernel(qc_ref, kc_ref, vc_ref, uc_ref, ql_ref, kl_ref, vl_ref, ul_ref, bias_ref, pw_ref, ps_ref,
                    oc_ref, ol_ref):
    lane = lax.broadcasted_iota(jnp.int32, (1, 2 * A_DH), 1)
    first = lane < A_DH
    pairs = [slice(p * 2 * A_DH, (p + 1) * 2 * A_DH) for p in range(A_HEADS // 2)]

    qs = [_stack_heads(qc_ref[:, cols], first) for cols in pairs]
    s_all = jnp.concatenate([_dot_nt(q, kc_ref[:, cols]) for q, cols in zip(qs, pairs)], axis=0)
    (p_all,), inv = _softmax_parts([s_all])
    for i, cols in enumerate(pairs):
        rows = slice(i * 2 * CTX, (i + 1) * 2 * CTX)
        o = _dot(p_all[rows], vc_ref[:, cols]) * inv[rows]
        oc_ref[:, cols] = _unstack_heads(o, first).astype(oc_ref.dtype)

    def row_body(r, carry):
        r0 = jnp.clip(r - WIN_R // 2, 0, ROWS - WIN_R)
        qrows = pl.ds(pl.multiple_of(r * GRID_W, GRID_W), GRID_W)
        krows = pl.ds(pl.multiple_of(r0 * GRID_W, GRID_W), WIN_R * GRID_W)
        qs = [_stack_heads(ql_ref[qrows, cols], first) for cols in pairs]
        s_loc = jnp.concatenate([_dot_nt(q, kl_ref[krows, cols]) for q, cols in zip(qs, pairs)], axis=0)
        s_ctx = jnp.concatenate([_dot_nt(q, kc_ref[:, cols]) for q, cols in zip(qs, pairs)], axis=0)
        (p_loc, p_ctx), inv = _softmax_parts([s_loc + bias_ref[r - r0], s_ctx])
        for i, cols in enumerate(pairs):
            rows = slice(i * 2 * GRID_W, (i + 1) * 2 * GRID_W)
            o = (_dot(p_loc[rows], vl_ref[krows, cols]) + _dot(p_ctx[rows], vc_ref[:, cols])) * inv[rows]
            ol_ref[qrows, cols] = _unstack_heads(o, first).astype(ol_ref.dtype)
        return carry

    lax.fori_loop(0, ROWS, row_body, 0, unroll=2)

    _pool_mix(uc_ref, pw_ref, ps_ref, oc_ref, CTX)
    _pool_mix(ul_ref, pw_ref, ps_ref, ol_ref, SEQ)


def _ab_core(qc, kc, vc, uc, ql, kl, vl, ul, bias, pool_w, pool_scale):
    cspec = lambda w: pl.BlockSpec((CTX, w), lambda b: (b, 0))
    lspec = lambda w: pl.BlockSpec((SEQ, w), lambda b: (b, 0))
    return pl.pallas_call(
        _ab_core_kernel,
        grid=(BATCH,),
        in_specs=[
            cspec(A_WIDTH), cspec(A_WIDTH), cspec(A_WIDTH), cspec(B_WIDTH),
            lspec(A_WIDTH), lspec(A_WIDTH), lspec(A_WIDTH), lspec(B_WIDTH),
            _const_spec((WIN_R, A_HEADS * GRID_W, WIN_R * GRID_W)),
            _const_spec((len(POOL_WINDOWS), B_GROUP, B_GROUP)),
            _const_spec((1, B_WIDTH)),
        ],
        out_specs=[cspec(D), lspec(D)],
        out_shape=[
            jax.ShapeDtypeStruct((BATCH * CTX, D), BF16),
            jax.ShapeDtypeStruct((BATCH * SEQ, D), BF16),
        ],
        compiler_params=_params(1),
        name="ab_core",
    )(qc, kc, vc, uc, ql, kl, vl, ul, bias, pool_w, pool_scale)


HG_SMALL_LEVELS = (2, 4)


def _hg_constants():
    t = jnp.arange(HG_CHUNK)
    low = (t[:, None] >= t[None, :]).astype(BF16)
    cum = jnp.stack([jnp.concatenate([low, low], axis=1), jnp.concatenate([low.T, low.T], axis=1)])
    upper = jnp.stack([(t & c) != 0 for c in HG_SMALL_LEVELS])
    sgn_f = jnp.where(upper, 1.0, -1.0).astype(F32)
    sgn = jnp.stack([sgn_f, -sgn_f])[:, :, :, None]
    return cum, jnp.broadcast_to(sgn, (2, len(HG_SMALL_LEVELS), HG_CHUNK, HG_DK))


def _bcast_row(x3, k):
    return jnp.broadcast_to(x3[:, k:k + 1, :], x3.shape)


def _hg_scores(q, k, g, sgn_ref, level_of, reverse):
    n8, n16 = HG_CHUNK // 8, HG_CHUNK // 16
    row = lax.broadcasted_iota(jnp.int32, (HG_CHUNK, HG_CHUNK), 0)
    colm = lax.broadcasted_iota(jnp.int32, (HG_CHUNK, HG_CHUNK), 1)
    a = jnp.where(row == colm, jnp.sum(q * k, axis=-1, keepdims=True), 0.0)
    qb = q.astype(BF16)
    kb = k.astype(BF16)
    a = jnp.where(level_of == 0, _dot_nt(qb * (1.0 - k).astype(BF16), kb), a)

    g3 = g.reshape(n8, 8, HG_DK)
    sub = lax.broadcasted_iota(jnp.int32, g3.shape, 1)
    pick = (lambda r: r + 1) if reverse else (lambda r: r)
    small_refs = (jnp.where(sub < 4, _bcast_row(g3, pick(1)), _bcast_row(g3, pick(5))), _bcast_row(g3, pick(3)))
    for i, ref in enumerate(small_refs):
        w = jnp.exp2((g - ref.reshape(HG_CHUNK, HG_DK)) * sgn_ref[i]).astype(BF16)
        a = jnp.where(level_of == i + 1, _dot_nt(qb * w, kb * w), a)

    edge = _bcast_row(g3, 0 if reverse else 7)
    qb3, kb3 = qb.reshape(n16, 16, HG_DK), kb.reshape(n16, 16, HG_DK)
    lvl3 = level_of.reshape(n16, 16, HG_CHUNK)
    for lvl, c in [(l, 1 << l) for l in range(3, HG_CHUNK.bit_length() - 1)]:
        span = 2 * c // 8
        dist = []
        for j in range(n8):
            b = edge[(j // span) * span + span // 2 - (0 if reverse else 1)]
            upper = j % span >= span // 2
            dist.append(g3[j] - b if upper != reverse else b - g3[j])
        w = jnp.exp2(jnp.concatenate(dist, axis=0)).astype(BF16)
        if c == 8:
            a = jnp.where(level_of == lvl, _dot_nt(qb * w, kb * w), a)
            continue
        w3 = w.reshape(n16, 16, HG_DK)
        span16 = span // 2
        is_query = [(j % span16 >= span16 // 2) != reverse for j in range(n16)]
        lhs = jnp.concatenate([qb3[j] * w3[j] for j in range(n16) if is_query[j]], axis=0)
        rhs = jnp.concatenate([kb3[j] if is_query[j] else kb3[j] * w3[j] for j in range(n16)], axis=0)
        a_l = _dot_nt(lhs, rhs).reshape(n16 // 2, 16, HG_CHUNK)
        a3 = a.reshape(n16, 16, HG_CHUNK)
        groups, u = [], 0
        for j in range(n16):
            if is_query[j]:
                groups.append(jnp.where(lvl3[j] == lvl, a_l[u], a3[j]))
                u += 1
            else:
                groups.append(a3[j])
        a = jnp.concatenate(groups, axis=0)
    return a.astype(BF16)


def _hg_state(k, v, g, st, reverse):
    g_end = g[0:1, :] if reverse else g[HG_CHUNK - 1:HG_CHUNK, :]
    kp = (k * jnp.exp2(g_end - g)).astype(BF16)
    return st * jnp.exp2(g_end) + _dot_tn(v.astype(BF16), kp)


def _hg_out(q, v, g, st, a):
    qp = (q * jnp.exp2(g)).astype(BF16)
    return _dot_nt(qp, st.astype(BF16)) + _dot(a, v.astype(BF16))


def _hg_kernel(xc_ref, xl_ref, w_ref, lb_ref, gn_ref, cum_ref, sgn_ref, o_ref,
               q_s, v_s, kf_s, kb_s, gf_s, gb_s, of_s, ob_s, a_s):
    n_ctx_chunks = CTX // HG_CHUNK
    n_lat_chunks = SEQ // HG_CHUNK
    heads = range(HG_STEP_HEADS)
    dirs = ((False, kf_s, gf_s, of_s), (True, kb_s, gb_s, ob_s))

    def project(x_ref, base, n):
        p = _dot(x_ref[...], w_ref[...])
        rows = slice(base, base + n)
        gates = []
        for hh in heads:
            c0 = hh * HG_COLS
            q_s[hh, rows, :] = _silu(p[:, c0:c0 + HG_DK])
            v_s[hh, rows, :] = p[:, c0 + HG_DK:c0 + 2 * HG_DK]
            for d, (_, k_s, g_s, _) in enumerate(dirs):
                lbd = lb_ref[d:d + 1, hh * HG_DK:(hh + 1) * HG_DK]
                f = lbd + (1.0 - lbd) * _sigmoid(p[:, c0 + (2 + d) * HG_DK:c0 + (3 + d) * HG_DK])
                k_s[hh, rows, :] = 1.0 - f
                lf = jnp.log2(f)
                hi = lf.astype(BF16)
                lo = (lf - hi.astype(F32)).astype(BF16)
                for i in range(n // HG_CHUNK):
                    blk = slice(i * HG_CHUNK, (i + 1) * HG_CHUNK)
                    g_s[hh, base + i * HG_CHUNK:base + (i + 1) * HG_CHUNK, :] = _dot(
                        cum_ref[d], jnp.concatenate([hi[blk], lo[blk]], axis=0))
            gates.append(p[:, c0 + 4 * HG_DK:c0 + 5 * HG_DK])
        return gates

    project(xc_ref, 0, CTX)
    gates = project(xl_ref, CTX, SEQ)

    row = lax.broadcasted_iota(jnp.int32, (HG_CHUNK, HG_CHUNK), 0)
    colm = lax.broadcasted_iota(jnp.int32, (HG_CHUNK, HG_CHUNK), 1)
    diff = row ^ colm
    lvl = jnp.zeros_like(diff)
    for c in range(1, HG_CHUNK.bit_length() - 1):
        lvl = lvl + (diff >= (1 << c)).astype(jnp.int32)
    lvl_of = (jnp.where(row > colm, lvl, -1), jnp.where(row < colm, lvl, -1))

    def operands(hh, d, base):
        rows = pl.ds(base, HG_CHUNK)
        _, k_s, g_s, _ = dirs[d]
        return q_s[hh, rows, :], k_s[hh, rows, :], v_s[hh, rows, :], g_s[hh, rows, :]

    sts = [jnp.zeros((HG_DK, HG_DK), F32) for _ in heads for _ in dirs]
    for i in range(n_ctx_chunks):
        for hh in heads:
            for d, (reverse, *_) in enumerate(dirs):
                j = (n_ctx_chunks - 1 - i) if reverse else i
                _, k, v, g = operands(hh, d, j * HG_CHUNK)
                sts[2 * hh + d] = _hg_state(k, v, g, sts[2 * hh + d], reverse)

    def lat_base(i, reverse):
        j = (n_lat_chunks - 1 - i) if reverse else i
        return pl.multiple_of(j * HG_CHUNK, HG_CHUNK)

    def scores(i):
        for hh in heads:
            for d, (reverse, *_) in enumerate(dirs):
                q, k, _, g = operands(hh, d, CTX + lat_base(i, reverse))
                a_s[hh, d, i] = _hg_scores(q, k, g, sgn_ref.at[d], lvl_of[d], reverse)

    def body(i, carry):
        sts = list(carry)
        for hh in heads:
            for d, (reverse, _, _, o_s) in enumerate(dirs):
                base = lat_base(i, reverse)
                q, k, v, g = operands(hh, d, CTX + base)
                o_s[hh, pl.ds(base, HG_CHUNK), :] = _hg_out(q, v, g, sts[2 * hh + d], a_s[hh, d, i])
                sts[2 * hh + d] = _hg_state(k, v, g, sts[2 * hh + d], reverse)
        scores(jnp.minimum(i + 1, n_lat_chunks - 1))
        return tuple(sts)

    scores(0)
    lax.fori_loop(0, n_lat_chunks, body, tuple(sts), unroll=2)

    for hh in heads:
        o = of_s[hh] + ob_s[hh]
        o = o * lax.rsqrt(jnp.mean(o * o, axis=-1, keepdims=True) + EPS) * gn_ref[...]
        o_ref[:, hh * HG_DK:(hh + 1) * HG_DK] = (o * _silu(gates[hh])).astype(o_ref.dtype)


def _hg_core(xn_ctx, xn_lat, w_in, lb, gnorm):
    nh = HG_STEP_HEADS
    steps = HG_HEADS // nh
    w_steps = (w_in.reshape(D, 5, steps, nh, HG_DK).transpose(2, 0, 3, 1, 4).reshape(steps, D, nh * HG_COLS))
    seq_scr = pltpu.VMEM((nh, CTX + SEQ, HG_DK), F32)
    lat_scr = pltpu.VMEM((nh, SEQ, HG_DK), F32)
    a_scr = pltpu.VMEM((nh, 2, SEQ // HG_CHUNK, HG_CHUNK, HG_CHUNK), BF16)
    cum, sgn = _hg_constants()
    return pl.pallas_call(
        _hg_kernel,
        grid=(BATCH, steps),
        in_specs=[
            pl.BlockSpec((CTX, D), lambda b, h: (b, 0)),
            pl.BlockSpec((SEQ, D), lambda b, h: (b, 0)),
            pl.BlockSpec((None, D, nh * HG_COLS), lambda b, h: (h, 0, 0)),
            pl.BlockSpec((2, nh * HG_DK), lambda b, h: (0, h)),
            pl.BlockSpec((1, HG_DK), lambda b, h: (0, 0)),
            _const_spec(cum.shape),
            _const_spec(sgn.shape),
        ],
        out_specs=pl.BlockSpec((SEQ, nh * HG_DK), lambda b, h: (b, h)),
        out_shape=jax.ShapeDtypeStruct((BATCH * SEQ, HG_KDIM), BF16),
        scratch_shapes=[seq_scr] * 6 + [lat_scr] * 2 + [a_scr],
        compiler_params=_params(2),
        name="hgrn2",
    )(xn_ctx, xn_lat, w_steps, lb, gnorm, cum, sgn)


def kernel(x, c, ctx, c_ctx, w_mod, b_mod, norm_g, ffn_w13, ffn_w2, ab_w_in, ab_rpb, ab_pool_w, ab_pool_scale,
           ab_w_out, hg_w_in, hg_lb_logits, hg_gnorm, hg_w_out, final_g):
    h_lat = x.reshape(BATCH * SEQ, D)
    h_ctx = ctx.reshape(BATCH * CTX, D)

    cvec = jnp.zeros((MOD_ROWS, D), F32).at[0].set(c_ctx).at[1:1 + BATCH].set(c)
    mod = _modulation(cvec, w_mod, b_mod)

    w13 = ffn_w13.astype(BF16)
    w2 = ffn_w2.astype(BF16)
    ffn = functools.partial(_rows, mod=mod, norm_g=norm_g, w13=w13, w2=w2)

    w_in = ab_w_in[0].astype(BF16)
    h_lat, ql, kl, vl, ul = ffn(h_lat, layer=0, sub=0, is_ctx=False, post="ab", w_post=w_in)
    h_ctx, qc, kc, vc, uc = ffn(h_ctx, layer=0, sub=0, is_ctx=True, post="ab", w_post=w_in)
    mix_ctx, mix_lat = _ab_core(qc, kc, vc, uc, ql, kl, vl, ul, _attn_bias_table(ab_rpb[0]),
                                ab_pool_w[0].astype(BF16), ab_pool_scale[0].reshape(1, B_WIDTH))
    w_out = ab_w_out[0].astype(BF16)
    (h_lat,) = ffn(h_lat, layer=0, sub=2, is_ctx=False, mix=(mix_lat, w_out))
    (h_ctx,) = ffn(h_ctx, layer=0, sub=2, is_ctx=True, mix=(mix_ctx, w_out))

    h_lat, xn_lat = ffn(h_lat, layer=1, sub=0, is_ctx=False, post="xn")
    _, xn_ctx = ffn(h_ctx, layer=1, sub=0, is_ctx=True, post="xn")
    sm = jax.nn.softmax(hg_lb_logits.astype(F32), axis=0)
    lb = (jnp.cumsum(sm, axis=0) - sm[0:1])[1]
    o = _hg_core(xn_ctx, xn_lat, hg_w_in[0].astype(BF16), lb, hg_gnorm[0].reshape(1, HG_DK))
    (out,) = ffn(h_lat, layer=1, sub=2, is_ctx=False, mix=(o, hg_w_out[0].astype(BF16)), post="final",
                 w_post=final_g)
    return out.reshape(BATCH, SEQ, D)
```

```python
import functools

import jax
import jax.numpy as jnp
from jax import lax
from jax.experimental import pallas as pl
from jax.experimental.pallas import tpu as pltpu

D = 1024
BATCH = 16
SEQ = 2048
DEPTH = 2
GRID_W = 64
CTX = 256
EPS = 1e-6
N_MOD = 9
D_FF = 2816
A_HEADS = 8
A_DH = 64
A_WIDTH = 512
WIN_R = 8
WIN_C = 16
POOL_WINDOWS = (2, 4, 8, 16)
B_WIDTH = 512
B_GROUP = 128
HG_HEADS = 8
HG_DK = 128
HG_KDIM = 1024

ROWS = SEQ // GRID_W
MOD_ROWS = 24
ROW_TILE = 1024
ROW_TILE_AB = 512
FF_CHUNK = 256
HG_CHUNK = 128
HG_STEP_HEADS = 2
NEG_BIG = -1e30
VMEM_LIMIT = 56 * 1024 * 1024

F32 = jnp.float32
BF16 = jnp.bfloat16


def _dot(a, b):
    return jnp.dot(a, b, preferred_element_type=F32)


def _dot_nt(a, b):
    return lax.dot_general(a, b, (((1,), (1,)), ((), ())), preferred_element_type=F32)


def _dot_tn(a, b):
    return lax.dot_general(a, b, (((0,), (0,)), ((), ())), preferred_element_type=F32)


def _sigmoid(x):
    return 1.0 / (1.0 + jnp.exp(-x))


def _silu(x):
    return x * _sigmoid(x)


def _params(n_axes):
    return pltpu.CompilerParams(dimension_semantics=("arbitrary",) * n_axes, vmem_limit_bytes=VMEM_LIMIT)


def _const_spec(shape):
    nd = len(shape)
    return pl.BlockSpec(shape, lambda *_: (0,) * nd, pipeline_mode=pl.Buffered(1))


def _adaln(x, g, shift, scale):
    ms = jnp.mean(x * x, axis=-1, keepdims=True)
    return (x * lax.rsqrt(ms + EPS)) * g * (1.0 + scale) + shift


def _mod_kernel(c_ref, w_ref, b_ref, o_ref):
    s = _silu(c_ref[...]).astype(BF16)
    o_ref[...] = _dot(s, w_ref[...].astype(BF16)) + b_ref[...]


def _modulation(cvec, w_mod, b_mod):
    nblk = N_MOD
    out = pl.pallas_call(
        _mod_kernel,
        grid=(DEPTH, nblk),
        in_specs=[
            pl.BlockSpec((MOD_ROWS, D), lambda l, j: (0, 0)),
            pl.BlockSpec((None, D, D), lambda l, j: (l, 0, j)),
            pl.BlockSpec((None, 1, D), lambda l, j: (l, 0, j)),
        ],
        out_specs=pl.BlockSpec((None, MOD_ROWS, D), lambda l, j: (l, 0, j)),
        out_shape=jax.ShapeDtypeStruct((DEPTH, MOD_ROWS, N_MOD * D), F32),
        compiler_params=_params(2),
        name="modulation",
    )(cvec, w_mod, b_mod.reshape(DEPTH, 1, N_MOD * D))
    return out.reshape(DEPTH, MOD_ROWS, N_MOD, D)


def _mod_spec(layer, is_ctx, tile):
    tiles_per_batch = SEQ // tile
    if is_ctx:
        return pl.BlockSpec((None, None, N_MOD, D), lambda t: (layer, 0, 0, 0))
    return pl.BlockSpec((None, None, N_MOD, D), lambda t: (layer, 1 + t // tiles_per_batch, 0, 0))


def _norm_spec(layer):
    return pl.BlockSpec((None, 3, D), lambda t: (layer, 0, 0))


def _row_spec(width, tile):
    return pl.BlockSpec((tile, width), lambda t: (t, 0))


def _rows_kernel(sub, pre_mix, post, *refs):
    refs = list(refs)
    h_ref, mod_ref, g_ref = refs[:3]
    del refs[:3]
    if pre_mix:
        x_ref, wmix_ref = refs[:2]
        del refs[:2]
    w13_ref, w2_ref = refs[:2]
    del refs[:2]
    if post in ("ab", "final"):
        wpost_ref = refs.pop(0)
    o_ref = refs.pop(0)

    h = h_ref[...]
    if pre_mix:
        h = h + mod_ref[5:6, :] * _dot(x_ref[...], wmix_ref[...])
    shift = mod_ref[3 * sub:3 * sub + 1, :]
    scale = mod_ref[3 * sub + 1:3 * sub + 2, :]
    gate = mod_ref[3 * sub + 2:3 * sub + 3, :]
    xb = _adaln(h, g_ref[sub:sub + 1, :], shift, scale).astype(BF16)
    acc = jnp.zeros(h.shape, F32)
    for c in range(D_FF // FF_CHUNK):
        lo = c * FF_CHUNK
        a = _dot(xb, w13_ref[:, lo:lo + FF_CHUNK])
        b = _dot(xb, w13_ref[:, D_FF + lo:D_FF + lo + FF_CHUNK])
        act = (_silu(a) * b).astype(BF16)
        acc = acc + _dot(act, w2_ref[lo:lo + FF_CHUNK, :])
    y = h + (0.5 * gate) * acc
    if post == "final":
        ms = jnp.mean(y * y, axis=-1, keepdims=True)
        y = (y * lax.rsqrt(ms + EPS)) * wpost_ref[...]
    o_ref[...] = y
    if post in ("ab", "xn"):
        xm = _adaln(y, g_ref[1:2, :], mod_ref[3:4, :], mod_ref[4:5, :]).astype(BF16)
    if post == "xn":
        refs[0][...] = xm
    if post == "ab":
        q_ref, k_ref, v_ref, u_ref = refs
        q_ref[...] = (_dot(xm, wpost_ref[:, 0:A_WIDTH]) * (A_DH ** -0.5)).astype(BF16)
        k_ref[...] = _dot(xm, wpost_ref[:, A_WIDTH:2 * A_WIDTH]).astype(BF16)
        v_ref[...] = _dot(xm, wpost_ref[:, 2 * A_WIDTH:3 * A_WIDTH]).astype(BF16)
        u_ref[...] = _dot(xm, wpost_ref[:, 3 * A_WIDTH:])


def _rows(h, mod, norm_g, w13, w2, layer, sub, is_ctx, mix=None, post=None, w_post=None):
    n = h.shape[0]
    which = sub // 2
    tile = ROW_TILE_AB if post == "ab" else ROW_TILE
    row_spec = functools.partial(_row_spec, tile=tile)
    in_specs = [row_spec(D), _mod_spec(layer, is_ctx, tile), _norm_spec(layer)]
    args = [h, mod, norm_g]
    if mix is not None:
        in_specs += [row_spec(D), _const_spec((D, D))]
        args += list(mix)
    in_specs += [
        pl.BlockSpec((None, None, D, 2 * D_FF), lambda t: (layer, which, 0, 0), pipeline_mode=pl.Buffered(1)),
        pl.BlockSpec((None, None, D_FF, D), lambda t: (layer, which, 0, 0), pipeline_mode=pl.Buffered(1)),
    ]
    args += [w13, w2]
    out_specs = [row_spec(D)]
    out_shape = [jax.ShapeDtypeStruct((n, D), F32)]
    if post == "final":
        in_specs.append(_const_spec((1, D)))
        args.append(w_post.reshape(1, D))
    elif post == "ab":
        in_specs.append(_const_spec((D, 3 * A_WIDTH + B_WIDTH)))
        args.append(w_post)
        out_specs += [row_spec(A_WIDTH)] * 3 + [row_spec(B_WIDTH)]
        out_shape += [jax.ShapeDtypeStruct((n, A_WIDTH), BF16)] * 3 + [jax.ShapeDtypeStruct((n, B_WIDTH), F32)]
    elif post == "xn":
        out_specs.append(row_spec(D))
        out_shape.append(jax.ShapeDtypeStruct((n, D), BF16))
    return pl.pallas_call(
        functools.partial(_rows_kernel, sub, mix is not None, post),
        grid=(n // tile,),
        in_specs=in_specs,
        out_specs=out_specs,
        out_shape=out_shape,
        compiler_params=_params(1),
        name="rows",
    )(*args)


def _attn_bias_table(rpb):
    col = jnp.arange(GRID_W)
    win_c0 = jnp.clip(col - WIN_C // 2, 0, GRID_W - WIN_C)
    kc = jnp.arange(GRID_W)
    col_ok = (kc[None, :] >= win_c0[:, None]) & (kc[None, :] < win_c0[:, None] + WIN_C)
    rel_c = jnp.clip(kc[None, :] - col[:, None], -(WIN_C - 1), WIN_C - 1) + (WIN_C - 1)
    onehot = (rel_c[None] == jnp.arange(2 * WIN_C - 1)[:, None, None]).astype(F32)
    by_col = jnp.einsum("hri,iqk->hqrk", rpb.astype(F32), onehot, precision=lax.Precision.HIGHEST)
    by_col = jnp.where(col_ok[None, :, None, :], by_col, NEG_BIG)
    tab = jnp.stack([by_col[:, :, WIN_R - 1 - dr:2 * WIN_R - 1 - dr] for dr in range(WIN_R)])
    return tab.reshape(WIN_R, A_HEADS * GRID_W, WIN_R * GRID_W)


def _softmax_parts(s_parts):
    m = s_parts[0].max(axis=-1, keepdims=True)
    for s in s_parts[1:]:
        m = jnp.maximum(m, s.max(axis=-1, keepdims=True))
    ps = [jnp.exp(s - m) for s in s_parts]
    den = ps[0].sum(axis=-1, keepdims=True)
    for p in ps[1:]:
        den = den + p.sum(axis=-1, keepdims=True)
    return [p.astype(BF16) for p in ps], 1.0 / den


def _stack_heads(x2, first):
    zero = jnp.zeros((), x2.dtype)
    return jnp.concatenate([jnp.where(first, x2, zero), jnp.where(first, zero, x2)], axis=0)


def _unstack_heads(o, first):
    m = o.shape[0] // 2
    return jnp.where(first, o[:m], o[m:])


def _shift_rows(x, k, row, n):
    if k > 0:
        return jnp.where(row >= k, pltpu.roll(x, k, 0), 0.0)
    return jnp.where(row < n + k, pltpu.roll(x, n + k, 0), 0.0)


def _pool_mix(u_ref, pw_ref, ps_ref, o_ref, n):
    row = lax.broadcasted_iota(jnp.int32, (n, B_GROUP), 0)
    rowf = row.astype(F32)
    for gi, w in enumerate(POOL_WINDOWS):
        x = u_ref[:, gi * B_GROUP:(gi + 1) * B_GROUP]
        half = w // 2
        back = x
        fwd = x
        m = 1
        while m < half:
            back = back + _shift_rows(back, m, row, n)
            fwd = fwd + _shift_rows(fwd, -m, row, n)
            m *= 2
        total = _shift_rows(back, 1, row, n) + fwd
        cnt = jnp.minimum(rowf + half, float(n)) - jnp.maximum(rowf - half, 0.0)
        d = (total / cnt - x).astype(BF16)
        y = _dot(d, pw_ref[gi]) * ps_ref[:, gi * B_GROUP:(gi + 1) * B_GROUP]
        o_ref[:, A_WIDTH + gi * B_GROUP:A_WIDTH + (gi + 1) * B_GROUP] = y.astype(o_ref.dtype)


def _ab_core_kernel(qc_ref, kc_ref, vc_ref, uc_ref, ql_ref, kl_ref, vl_ref, ul_ref, bias_ref, pw_ref, ps_ref,
                    oc_ref, ol_ref):
    lane = lax.broadcasted_iota(jnp.int32, (1, 2 * A_DH), 1)
    first = lane < A_DH
    pairs = [slice(p * 2 * A_DH, (p + 1) * 2 * A_DH) for p in range(A_HEADS // 2)]

    qs = [_stack_heads(qc_ref[:, cols], first) for cols in pairs]
    s_all = jnp.concatenate([_dot_nt(q, kc_ref[:, cols]) for q, cols in zip(qs, pairs)], axis=0)
    (p_all,), inv = _softmax_parts([s_all])
    for i, cols in enumerate(pairs):
        rows = slice(i * 2 * CTX, (i + 1) * 2 * CTX)
        o = _dot(p_all[rows], vc_ref[:, cols]) * inv[rows]
        oc_ref[:, cols] = _unstack_heads(o, first).astype(oc_ref.dtype)

    def row_body(r, carry):
        r0 = jnp.clip(r - WIN_R // 2, 0, ROWS - WIN_R)
        qrows = pl.ds(pl.multiple_of(r * GRID_W, GRID_W), GRID_W)
        krows = pl.ds(pl.multiple_of(r0 * GRID_W, GRID_W), WIN_R * GRID_W)
        qs = [_stack_heads(ql_ref[qrows, cols], first) for cols in pairs]
        s_loc = jnp.concatenate([_dot_nt(q, kl_ref[krows, cols]) for q, cols in zip(qs, pairs)], axis=0)
        s_ctx = jnp.concatenate([_dot_nt(q, kc_ref[:, cols]) for q, cols in zip(qs, pairs)], axis=0)
        (p_loc, p_ctx), inv = _softmax_parts([s_loc + bias_ref[r - r0], s_ctx])
        for i, cols in enumerate(pairs):
            rows = slice(i * 2 * GRID_W, (i + 1) * 2 * GRID_W)
            o = (_dot(p_loc[rows], vl_ref[krows, cols]) + _dot(p_ctx[rows], vc_ref[:, cols])) * inv[rows]
            ol_ref[qrows, cols] = _unstack_heads(o, first).astype(ol_ref.dtype)
        return carry

    lax.fori_loop(0, ROWS, row_body, 0, unroll=2)

    _pool_mix(uc_ref, pw_ref, ps_ref, oc_ref, CTX)
    _pool_mix(ul_ref, pw_ref, ps_ref, ol_ref, SEQ)


def _ab_core(qc, kc, vc, uc, ql, kl, vl, ul, bias, pool_w, pool_scale):
    cspec = lambda w: pl.BlockSpec((CTX, w), lambda b: (b, 0))
    lspec = lambda w: pl.BlockSpec((SEQ, w), lambda b: (b, 0))
    return pl.pallas_call(
        _ab_core_kernel,
        grid=(BATCH,),
        in_specs=[
            cspec(A_WIDTH), cspec(A_WIDTH), cspec(A_WIDTH), cspec(B_WIDTH),
            lspec(A_WIDTH), lspec(A_WIDTH), lspec(A_WIDTH), lspec(B_WIDTH),
            _const_spec((WIN_R, A_HEADS * GRID_W, WIN_R * GRID_W)),
            _const_spec((len(POOL_WINDOWS), B_GROUP, B_GROUP)),
            _const_spec((1, B_WIDTH)),
        ],
        out_specs=[cspec(D), lspec(D)],
        out_shape=[
            jax.ShapeDtypeStruct((BATCH * CTX, D), BF16),
            jax.ShapeDtypeStruct((BATCH * SEQ, D), BF16),
        ],
        compiler_params=_params(1),
        name="ab_core",
    )(qc, kc, vc, uc, ql, kl, vl, ul, bias, pool_w, pool_scale)


HG_SMALL_LEVELS = (2, 4)


def _hg_constants():
    t = jnp.arange(HG_CHUNK)
    low = (t[:, None] >= t[None, :]).astype(BF16)
    cum = jnp.stack([jnp.concatenate([low, low], axis=1), jnp.concatenate([low.T, low.T], axis=1)])
    upper = jnp.stack([(t & c) != 0 for c in HG_SMALL_LEVELS])
    sgn_f = jnp.where(upper, 1.0, -1.0).astype(F32)
    sgn = jnp.stack([sgn_f, -sgn_f])[:, :, :, None]
    return cum, jnp.broadcast_to(sgn, (2, len(HG_SMALL_LEVELS), HG_CHUNK, HG_DK))


def _bcast_row(x3, k):
    return jnp.broadcast_to(x3[:, k:k + 1, :], x3.shape)


def _hg_scores(q, k, g, sgn_ref, level_of, reverse):
    n8, n16 = HG_CHUNK // 8, HG_CHUNK // 16
    row = lax.broadcasted_iota(jnp.int32, (HG_CHUNK, HG_CHUNK), 0)
    colm = lax.broadcasted_iota(jnp.int32, (HG_CHUNK, HG_CHUNK), 1)
    a = jnp.where(row == colm, jnp.sum(q * k, axis=-1, keepdims=True), 0.0)
    qb = q.astype(BF16)
    kb = k.astype(BF16)
    a = jnp.where(level_of == 0, _dot_nt(qb * (1.0 - k).astype(BF16), kb), a)

    g3 = g.reshape(n8, 8, HG_DK)
    sub = lax.broadcasted_iota(jnp.int32, g3.shape, 1)
    pick = (lambda r: r + 1) if reverse else (lambda r: r)
    small_refs = (jnp.where(sub < 4, _bcast_row(g3, pick(1)), _bcast_row(g3, pick(5))), _bcast_row(g3, pick(3)))
    for i, ref in enumerate(small_refs):
        w = jnp.exp2((g - ref.reshape(HG_CHUNK, HG_DK)) * sgn_ref[i]).astype(BF16)
        a = jnp.where(level_of == i + 1, _dot_nt(qb * w, kb * w), a)

    edge = _bcast_row(g3, 0 if reverse else 7)
    qb3, kb3 = qb.reshape(n16, 16, HG_DK), kb.reshape(n16, 16, HG_DK)
    lvl3 = level_of.reshape(n16, 16, HG_CHUNK)
    for lvl, c in [(l, 1 << l) for l in range(3, HG_CHUNK.bit_length() - 1)]:
        span = 2 * c // 8
        dist = []
        for j in range(n8):
            b = edge[(j // span) * span + span // 2 - (0 if reverse else 1)]
            upper = j % span >= span // 2
            dist.append(g3[j] - b if upper != reverse else b - g3[j])
        w = jnp.exp2(jnp.concatenate(dist, axis=0)).astype(BF16)
        if c == 8:
            a = jnp.where(level_of == lvl, _dot_nt(qb * w, kb * w), a)
            continue
        w3 = w.reshape(n16, 16, HG_DK)
        span16 = span // 2
        is_query = [(j % span16 >= span16 // 2) != reverse for j in range(n16)]
        lhs = jnp.concatenate([qb3[j] * w3[j] for j in range(n16) if is_query[j]], axis=0)
        rhs = jnp.concatenate([kb3[j] if is_query[j] else kb3[j] * w3[j] for j in range(n16)], axis=0)
        a_l = _dot_nt(lhs, rhs).reshape(n16 // 2, 16, HG_CHUNK)
        a3 = a.reshape(n16, 16, HG_CHUNK)
        groups, u = [], 0
        for j in range(n16):
            if is_query[j]:
                groups.append(jnp.where(lvl3[j] == lvl, a_l[u], a3[j]))
                u += 1
            else:
                groups.append(a3[j])
        a = jnp.concatenate(groups, axis=0)
    return a.astype(BF16)


def _hg_state(k, v, g, st, reverse):
    g_end = g[0:1, :] if reverse else g[HG_CHUNK - 1:HG_CHUNK, :]
    kp = (k * jnp.exp2(g_end - g)).astype(BF16)
    return st * jnp.exp2(g_end) + _dot_tn(v.astype(BF16), kp)


def _hg_out(q, v, g, st, a):
    qp = (q * jnp.exp2(g)).astype(BF16)
    return _dot_nt(qp, st.astype(BF16)) + _dot(a, v.astype(BF16))


def _hg_kernel(xc_ref, xl_ref, wq_ref, wi_ref, wff_ref, wfb_ref, wg_ref, lb_ref, gn_ref, cum_ref, sgn_ref, o_ref,
               q_s, v_s, kf_s, kb_s, gf_s, gb_s, of_s, ob_s, a_s):
    n_ctx_chunks = CTX // HG_CHUNK
    n_lat_chunks = SEQ // HG_CHUNK
    heads = range(HG_STEP_HEADS)
    dirs = ((False, kf_s, gf_s, of_s), (True, kb_s, gb_s, ob_s))

    def project(x_ref, base, n):
        x = x_ref[...]
        pq, pv, pff, pfb, pg = (_dot(x, w_ref[...]) for w_ref in (wq_ref, wi_ref, wff_ref, wfb_ref, wg_ref))
        rows = slice(base, base + n)
        gates = []
        for hh in heads:
            cols = slice(hh * HG_DK, (hh + 1) * HG_DK)
            q_s[hh, rows, :] = _silu(pq[:, cols])
            v_s[hh, rows, :] = pv[:, cols]
            for d, (_, k_s, g_s, _) in enumerate(dirs):
                lbd = lb_ref[d:d + 1, cols]
                f = lbd + (1.0 - lbd) * _sigmoid((pff, pfb)[d][:, cols])
                k_s[hh, rows, :] = 1.0 - f
                lf = jnp.log2(f)
                hi = lf.astype(BF16)
                lo = (lf - hi.astype(F32)).astype(BF16)
                for i in range(n // HG_CHUNK):
                    blk = slice(i * HG_CHUNK, (i + 1) * HG_CHUNK)
                    g_s[hh, base + i * HG_CHUNK:base + (i + 1) * HG_CHUNK, :] = _dot(
                        cum_ref[d], jnp.concatenate([hi[blk], lo[blk]], axis=0))
            gates.append(pg[:, cols])
        return gates

    project(xc_ref, 0, CTX)
    gates = project(xl_ref, CTX, SEQ)

    row = lax.broadcasted_iota(jnp.int32, (HG_CHUNK, HG_CHUNK), 0)
    colm = lax.broadcasted_iota(jnp.int32, (HG_CHUNK, HG_CHUNK), 1)
    diff = row ^ colm
    lvl = jnp.zeros_like(diff)
    for c in range(1, HG_CHUNK.bit_length() - 1):
        lvl = lvl + (diff >= (1 << c)).astype(jnp.int32)
    lvl_of = (jnp.where(row > colm, lvl, -1), jnp.where(row < colm, lvl, -1))

    def operands(hh, d, base):
        rows = pl.ds(base, HG_CHUNK)
        _, k_s, g_s, _ = dirs[d]
        return q_s[hh, rows, :], k_s[hh, rows, :], v_s[hh, rows, :], g_s[hh, rows, :]

    sts = [jnp.zeros((HG_DK, HG_DK), F32) for _ in heads for _ in dirs]
    for i in range(n_ctx_chunks):
        for hh in heads:
            for d, (reverse, *_) in enumerate(dirs):
                j = (n_ctx_chunks - 1 - i) if reverse else i
                _, k, v, g = operands(hh, d, j * HG_CHUNK)
                sts[2 * hh + d] = _hg_state(k, v, g, sts[2 * hh + d], reverse)

    def lat_base(i, reverse):
        j = (n_lat_chunks - 1 - i) if reverse else i
        return pl.multiple_of(j * HG_CHUNK, HG_CHUNK)

    def scores(i):
        for hh in heads:
            for d, (reverse, *_) in enumerate(dirs):
                q, k, _, g = operands(hh, d, CTX + lat_base(i, reverse))
                a_s[hh, d, i] = _hg_scores(q, k, g, sgn_ref.at[d], lvl_of[d], reverse)

    def body(i, carry):
        sts = list(carry)
        for hh in heads:
            for d, (reverse, _, _, o_s) in enumerate(dirs):
                base = lat_base(i, reverse)
                q, k, v, g = operands(hh, d, CTX + base)
                o_s[hh, pl.ds(base, HG_CHUNK), :] = _hg_out(q, v, g, sts[2 * hh + d], a_s[hh, d, i])
                sts[2 * hh + d] = _hg_state(k, v, g, sts[2 * hh + d], reverse)
        scores(jnp.minimum(i + 1, n_lat_chunks - 1))
        return tuple(sts)

    scores(0)
    lax.fori_loop(0, n_lat_chunks, body, tuple(sts), unroll=2)

    for hh in heads:
        o = of_s[hh] + ob_s[hh]
        o = o * lax.rsqrt(jnp.mean(o * o, axis=-1, keepdims=True) + EPS) * gn_ref[...]
        o_ref[:, hh * HG_DK:(hh + 1) * HG_DK] = (o * _silu(gates[hh])).astype(o_ref.dtype)


def _hg_core(xn_ctx, xn_lat, w_in, lb, gnorm):
    nh = HG_STEP_HEADS
    steps = HG_HEADS // nh
    seq_scr = pltpu.VMEM((nh, CTX + SEQ, HG_DK), F32)
    lat_scr = pltpu.VMEM((nh, SEQ, HG_DK), F32)
    a_scr = pltpu.VMEM((nh, 2, SEQ // HG_CHUNK, HG_CHUNK, HG_CHUNK), BF16)
    cum, sgn = _hg_constants()
    return pl.pallas_call(
        _hg_kernel,
        grid=(BATCH, steps),
        in_specs=[
            pl.BlockSpec((CTX, D), lambda b, h: (b, 0)),
            pl.BlockSpec((SEQ, D), lambda b, h: (b, 0)),
            *[pl.BlockSpec((D, nh * HG_DK), lambda b, h, j=j: (0, j * steps + h)) for j in range(5)],
            pl.BlockSpec((2, nh * HG_DK), lambda b, h: (0, h)),
            pl.BlockSpec((1, HG_DK), lambda b, h: (0, 0)),
            _const_spec(cum.shape),
            _const_spec(sgn.shape),
        ],
        out_specs=pl.BlockSpec((SEQ, nh * HG_DK), lambda b, h: (b, h)),
        out_shape=jax.ShapeDtypeStruct((BATCH * SEQ, HG_KDIM), BF16),
        scratch_shapes=[seq_scr] * 6 + [lat_scr] * 2 + [a_scr],
        compiler_params=_params(2),
        name="hgrn2",
    )(xn_ctx, xn_lat, *([w_in] * 5), lb, gnorm, cum, sgn)


def kernel(x, c, ctx, c_ctx, w_mod, b_mod, norm_g, ffn_w13, ffn_w2, ab_w_in, ab_rpb, ab_pool_w, ab_pool_scale,
           ab_w_out, hg_w_in, hg_lb_logits, hg_gnorm, hg_w_out, final_g):
    h_lat = x.reshape(BATCH * SEQ, D)
    h_ctx = ctx.reshape(BATCH * CTX, D)

    cvec = jnp.zeros((MOD_ROWS, D), F32).at[0].set(c_ctx).at[1:1 + BATCH].set(c)
    mod = _modulation(cvec, w_mod, b_mod)

    w13 = ffn_w13.astype(BF16)
    w2 = ffn_w2.astype(BF16)
    ffn = functools.partial(_rows, mod=mod, norm_g=norm_g, w13=w13, w2=w2)

    w_in = ab_w_in[0].astype(BF16)
    h_lat, ql, kl, vl, ul = ffn(h_lat, layer=0, sub=0, is_ctx=False, post="ab", w_post=w_in)
    h_ctx, qc, kc, vc, uc = ffn(h_ctx, layer=0, sub=0, is_ctx=True, post="ab", w_post=w_in)
    mix_ctx, mix_lat = _ab_core(qc, kc, vc, uc, ql, kl, vl, ul, _attn_bias_table(ab_rpb[0]),
                                ab_pool_w[0].astype(BF16), ab_pool_scale[0].reshape(1, B_WIDTH))
    w_out = ab_w_out[0].astype(BF16)
    (h_lat,) = ffn(h_lat, layer=0, sub=2, is_ctx=False, mix=(mix_lat, w_out))
    (h_ctx,) = ffn(h_ctx, layer=0, sub=2, is_ctx=True, mix=(mix_ctx, w_out))

    h_lat, xn_lat = ffn(h_lat, layer=1, sub=0, is_ctx=False, post="xn")
    _, xn_ctx = ffn(h_ctx, layer=1, sub=0, is_ctx=True, post="xn")
    sm = jax.nn.softmax(hg_lb_logits.astype(F32), axis=0)
    lb = (jnp.cumsum(sm, axis=0) - sm[0:1])[1]
    o = _hg_core(xn_ctx, xn_lat, hg_w_in[0].astype(BF16), lb, hg_gnorm[0].reshape(1, HG_DK))
    (out,) = ffn(h_lat, layer=1, sub=2, is_ctx=False, mix=(o, hg_w_out[0].astype(BF16)), post="final",
                 w_post=final_g)
    return out.reshape(BATCH, SEQ, D)
```

```python
import functools

import jax
import jax.numpy as jnp
from jax import lax
from jax.experimental import pallas as pl
from jax.experimental.pallas import tpu as pltpu

D = 1024
BATCH = 16
SEQ = 2048
DEPTH = 2
GRID_W = 64
CTX = 256
EPS = 1e-6
N_MOD = 9
D_FF = 2816
A_HEADS = 8
A_DH = 64
A_WIDTH = 512
WIN_R = 8
WIN_C = 16
POOL_WINDOWS = (2, 4, 8, 16)
B_WIDTH = 512
B_GROUP = 128
HG_HEADS = 8
HG_DK = 128
HG_KDIM = 1024

ROWS = SEQ // GRID_W
ROW_BLOCK = 8
MOD_ROWS = 24
ROW_TILE = 1024
ROW_TILE_AB = 512
FF_CHUNK = 256
HG_CHUNK = 128
HG_STEP_HEADS = 2
NEG_BIG = -1e30
VMEM_LIMIT = 56 * 1024 * 1024

F32 = jnp.float32
BF16 = jnp.bfloat16


def _dot(a, b):
    return jnp.dot(a, b, preferred_element_type=F32)


def _dot_nt(a, b):
    return lax.dot_general(a, b, (((1,), (1,)), ((), ())), preferred_element_type=F32)


def _dot_tn(a, b):
    return lax.dot_general(a, b, (((0,), (0,)), ((), ())), preferred_element_type=F32)


def _sigmoid(x):
    return 1.0 / (1.0 + jnp.exp(-x))


def _silu(x):
    return x * _sigmoid(x)


def _params(n_axes):
    return pltpu.CompilerParams(dimension_semantics=("arbitrary",) * n_axes, vmem_limit_bytes=VMEM_LIMIT)


def _const_spec(shape):
    nd = len(shape)
    return pl.BlockSpec(shape, lambda *_: (0,) * nd, pipeline_mode=pl.Buffered(1))


def _adaln(x, g, shift, scale):
    ms = jnp.mean(x * x, axis=-1, keepdims=True)
    return (x * lax.rsqrt(ms + EPS)) * (g * (1.0 + scale)) + shift


def _mod_kernel(c_ref, w_ref, b_ref, o_ref):
    s = _silu(c_ref[...]).astype(BF16)
    o_ref[...] = _dot(s, w_ref[...].astype(BF16)) + b_ref[...]


def _modulation(cvec, w_mod, b_mod):
    nblk = N_MOD
    out = pl.pallas_call(
        _mod_kernel,
        grid=(DEPTH, nblk),
        in_specs=[
            pl.BlockSpec((MOD_ROWS, D), lambda l, j: (0, 0)),
            pl.BlockSpec((None, D, D), lambda l, j: (l, 0, j)),
            pl.BlockSpec((None, 1, D), lambda l, j: (l, 0, j)),
        ],
        out_specs=pl.BlockSpec((None, MOD_ROWS, D), lambda l, j: (l, 0, j)),
        out_shape=jax.ShapeDtypeStruct((DEPTH, MOD_ROWS, N_MOD * D), F32),
        compiler_params=_params(2),
        name="modulation",
    )(cvec, w_mod, b_mod.reshape(DEPTH, 1, N_MOD * D))
    return out.reshape(DEPTH, MOD_ROWS, N_MOD, D)


def _mod_spec(layer, is_ctx, tile):
    tiles_per_batch = SEQ // tile
    if is_ctx:
        return pl.BlockSpec((None, None, N_MOD, D), lambda t: (layer, 0, 0, 0))
    return pl.BlockSpec((None, None, N_MOD, D), lambda t: (layer, 1 + t // tiles_per_batch, 0, 0))


def _norm_spec(layer):
    return pl.BlockSpec((None, 3, D), lambda t: (layer, 0, 0))


def _row_spec(width, tile):
    return pl.BlockSpec((tile, width), lambda t: (t, 0))


def _rows_kernel(sub, pre_mix, post, *refs):
    refs = list(refs)
    h_ref, mod_ref, g_ref = refs[:3]
    del refs[:3]
    if pre_mix:
        x_ref, wmix_ref = refs[:2]
        del refs[:2]
    w13_ref, w2_ref = refs[:2]
    del refs[:2]
    if post in ("ab", "final"):
        wpost_ref = refs.pop(0)
    o_ref = refs.pop(0)

    h = h_ref[...]
    if pre_mix:
        h = h + mod_ref[5:6, :] * _dot(x_ref[...], wmix_ref[...])
    shift = mod_ref[3 * sub:3 * sub + 1, :]
    scale = mod_ref[3 * sub + 1:3 * sub + 2, :]
    gate = mod_ref[3 * sub + 2:3 * sub + 3, :]
    xb = _adaln(h, g_ref[sub:sub + 1, :], shift, scale).astype(BF16)
    acc = jnp.zeros(h.shape, F32)
    for c in range(D_FF // FF_CHUNK):
        lo = c * FF_CHUNK
        a = _dot(xb, w13_ref[:, lo:lo + FF_CHUNK])
        b = _dot(xb, w13_ref[:, D_FF + lo:D_FF + lo + FF_CHUNK])
        act = (_silu(a) * b).astype(BF16)
        acc = acc + _dot(act, w2_ref[lo:lo + FF_CHUNK, :])
    y = h + (0.5 * gate) * acc
    if post == "final":
        ms = jnp.mean(y * y, axis=-1, keepdims=True)
        y = (y * lax.rsqrt(ms + EPS)) * wpost_ref[...]
    o_ref[...] = y
    if post in ("ab", "xn"):
        xm = _adaln(y, g_ref[1:2, :], mod_ref[3:4, :], mod_ref[4:5, :]).astype(BF16)
    if post == "xn":
        refs[0][...] = xm
    if post == "ab":
        q_ref, k_ref, v_ref, u_ref = refs
        q_ref[...] = (_dot(xm, wpost_ref[:, 0:A_WIDTH]) * (A_DH ** -0.5)).astype(BF16)
        k_ref[...] = _dot(xm, wpost_ref[:, A_WIDTH:2 * A_WIDTH]).astype(BF16)
        v_ref[...] = _dot(xm, wpost_ref[:, 2 * A_WIDTH:3 * A_WIDTH]).astype(BF16)
        u_ref[...] = _dot(xm, wpost_ref[:, 3 * A_WIDTH:])


def _rows(h, mod, norm_g, w13, w2, layer, sub, is_ctx, mix=None, post=None, w_post=None):
    n = h.shape[0]
    which = sub // 2
    tile = ROW_TILE_AB if post == "ab" else ROW_TILE
    row_spec = functools.partial(_row_spec, tile=tile)
    in_specs = [row_spec(D), _mod_spec(layer, is_ctx, tile), _norm_spec(layer)]
    args = [h, mod, norm_g]
    if mix is not None:
        in_specs += [row_spec(D), _const_spec((D, D))]
        args += list(mix)
    in_specs += [
        pl.BlockSpec((None, None, D, 2 * D_FF), lambda t: (layer, which, 0, 0), pipeline_mode=pl.Buffered(1)),
        pl.BlockSpec((None, None, D_FF, D), lambda t: (layer, which, 0, 0), pipeline_mode=pl.Buffered(1)),
    ]
    args += [w13, w2]
    out_specs = [row_spec(D)]
    out_shape = [jax.ShapeDtypeStruct((n, D), F32)]
    if post == "final":
        in_specs.append(_const_spec((1, D)))
        args.append(w_post.reshape(1, D))
    elif post == "ab":
        in_specs.append(_const_spec((D, 3 * A_WIDTH + B_WIDTH)))
        args.append(w_post)
        out_specs += [row_spec(A_WIDTH)] * 3 + [row_spec(B_WIDTH)]
        out_shape += [jax.ShapeDtypeStruct((n, A_WIDTH), BF16)] * 3 + [jax.ShapeDtypeStruct((n, B_WIDTH), F32)]
    elif post == "xn":
        out_specs.append(row_spec(D))
        out_shape.append(jax.ShapeDtypeStruct((n, D), BF16))
    return pl.pallas_call(
        functools.partial(_rows_kernel, sub, mix is not None, post),
        grid=(n // tile,),
        in_specs=in_specs,
        out_specs=out_specs,
        out_shape=out_shape,
        compiler_params=_params(1),
        name="rows",
    )(*args)


def _attn_bias_table(rpb):
    col = jnp.arange(GRID_W)
    win_c0 = jnp.clip(col - WIN_C // 2, 0, GRID_W - WIN_C)
    kc = jnp.arange(GRID_W)
    col_ok = (kc[None, :] >= win_c0[:, None]) & (kc[None, :] < win_c0[:, None] + WIN_C)
    rel_c = jnp.clip(kc[None, :] - col[:, None], -(WIN_C - 1), WIN_C - 1) + (WIN_C - 1)
    onehot = (rel_c[None] == jnp.arange(2 * WIN_C - 1)[:, None, None]).astype(F32)
    by_col = jnp.einsum("hri,iqk->hqrk", rpb.astype(F32), onehot, precision=lax.Precision.HIGHEST)
    by_col = jnp.where(col_ok[None, :, None, :], by_col, NEG_BIG)
    tab = jnp.stack([by_col[:, :, WIN_R - 1 - dr:2 * WIN_R - 1 - dr] for dr in range(WIN_R)])
    return tab.reshape(WIN_R, A_HEADS * GRID_W, WIN_R * GRID_W)


def _softmax_parts(s_parts):
    m = s_parts[0].max(axis=-1, keepdims=True)
    for s in s_parts[1:]:
        m = jnp.maximum(m, s.max(axis=-1, keepdims=True))
    ps = [jnp.exp(s - m) for s in s_parts]
    den = ps[0].sum(axis=-1, keepdims=True)
    for p in ps[1:]:
        den = den + p.sum(axis=-1, keepdims=True)
    return [p.astype(BF16) for p in ps], 1.0 / den


def _stack_heads(x2, first):
    zero = jnp.zeros((), x2.dtype)
    return jnp.concatenate([jnp.where(first, x2, zero), jnp.where(first, zero, x2)], axis=0)


def _unstack_heads(o, first):
    m = o.shape[0] // 2
    return jnp.where(first, o[:m], o[m:])


def _shift_rows(x, k, row, n):
    if k > 0:
        return jnp.where(row >= k, pltpu.roll(x, k, 0), 0.0)
    return jnp.where(row < n + k, pltpu.roll(x, n + k, 0), 0.0)


def _pool_mix(u_ref, pw_ref, ps_ref, o_ref, n):
    row = lax.broadcasted_iota(jnp.int32, (n, B_GROUP), 0)
    rowf = row.astype(F32)
    for gi, w in enumerate(POOL_WINDOWS):
        x = u_ref[:, gi * B_GROUP:(gi + 1) * B_GROUP]
        half = w // 2
        back = x
        fwd = x
        m = 1
        while m < half:
            back = back + _shift_rows(back, m, row, n)
            fwd = fwd + _shift_rows(fwd, -m, row, n)
            m *= 2
        total = _shift_rows(back, 1, row, n) + fwd
        cnt = jnp.minimum(rowf + half, float(n)) - jnp.maximum(rowf - half, 0.0)
        d = (total / cnt - x).astype(BF16)
        y = _dot(d, pw_ref[gi]) * ps_ref[:, gi * B_GROUP:(gi + 1) * B_GROUP]
        o_ref[:, A_WIDTH + gi * B_GROUP:A_WIDTH + (gi + 1) * B_GROUP] = y.astype(o_ref.dtype)


def _ab_core_kernel(qc_ref, kc_ref, vc_ref, uc_ref, ql_ref, kl_ref, vl_ref, ul_ref, bias_ref, pw_ref, ps_ref,
                    oc_ref, ol_ref, qs_s, sc_s, pc_s, ol_s, inv_s):
    lane = lax.broadcasted_iota(jnp.int32, (1, 2 * A_DH), 1)
    first = lane < A_DH
    pairs = [slice(p * 2 * A_DH, (p + 1) * 2 * A_DH) for p in range(A_HEADS // 2)]

    qs = [_stack_heads(qc_ref[:, cols], first) for cols in pairs]
    s_all = jnp.concatenate([_dot_nt(q, kc_ref[:, cols]) for q, cols in zip(qs, pairs)], axis=0)
    (p_all,), inv = _softmax_parts([s_all])
    for i, cols in enumerate(pairs):
        rows = slice(i * 2 * CTX, (i + 1) * 2 * CTX)
        o = _dot(p_all[rows], vc_ref[:, cols]) * inv[rows]
        oc_ref[:, cols] = _unstack_heads(o, first).astype(oc_ref.dtype)

    n_pairs = len(pairs)
    blk_q = ROW_BLOCK * GRID_W
    per_row = 2 * GRID_W

    def block_body(blk, carry):
        q0 = pl.multiple_of(blk * blk_q, blk_q)
        for i, cols in enumerate(pairs):
            for j in range(ROW_BLOCK):
                q = _stack_heads(ql_ref[pl.ds(q0 + j * GRID_W, GRID_W), cols], first)
                qs_s[i, j * per_row:(j + 1) * per_row, :] = q
            sc_s[i] = _dot_nt(qs_s[i], kc_ref[:, cols])

        def row_body(j, carry):
            r = blk * ROW_BLOCK + j
            r0 = jnp.clip(r - WIN_R // 2, 0, ROWS - WIN_R)
            krows = pl.ds(pl.multiple_of(r0 * GRID_W, GRID_W), WIN_R * GRID_W)
            rows_j = pl.ds(pl.multiple_of(j * per_row, per_row), per_row)
            s_loc = jnp.concatenate([_dot_nt(qs_s[i, rows_j, :], kl_ref[krows, cols])
                                     for i, cols in enumerate(pairs)], axis=0)
            s_ctx = jnp.concatenate([sc_s[i, rows_j, :] for i in range(n_pairs)], axis=0)
            (p_loc, p_ctx), inv = _softmax_parts([s_loc + bias_ref[r - r0], s_ctx])
            for i, cols in enumerate(pairs):
                rows = slice(i * per_row, (i + 1) * per_row)
                pc_s[i, rows_j, :] = p_ctx[rows]
                ol_s[i, rows_j, :] = _dot(p_loc[rows], vl_ref[krows, cols])
                inv_s[i, rows_j, :] = inv[rows]
            return carry

        lax.fori_loop(0, ROW_BLOCK, row_body, 0, unroll=2)

        for i, cols in enumerate(pairs):
            o = (ol_s[i] + _dot(pc_s[i], vc_ref[:, cols])) * inv_s[i]
            for j in range(ROW_BLOCK):
                ol_ref[pl.ds(q0 + j * GRID_W, GRID_W), cols] = _unstack_heads(
                    o[j * per_row:(j + 1) * per_row], first).astype(ol_ref.dtype)
        return carry

    lax.fori_loop(0, ROWS // ROW_BLOCK, block_body, 0)

    _pool_mix(uc_ref, pw_ref, ps_ref, oc_ref, CTX)
    _pool_mix(ul_ref, pw_ref, ps_ref, ol_ref, SEQ)


def _ab_core(qc, kc, vc, uc, ql, kl, vl, ul, bias, pool_w, pool_scale):
    cspec = lambda w: pl.BlockSpec((CTX, w), lambda b: (b, 0))
    lspec = lambda w: pl.BlockSpec((SEQ, w), lambda b: (b, 0))
    return pl.pallas_call(
        _ab_core_kernel,
        grid=(BATCH,),
        in_specs=[
            cspec(A_WIDTH), cspec(A_WIDTH), cspec(A_WIDTH), cspec(B_WIDTH),
            lspec(A_WIDTH), lspec(A_WIDTH), lspec(A_WIDTH), lspec(B_WIDTH),
            _const_spec((WIN_R, A_HEADS * GRID_W, WIN_R * GRID_W)),
            _const_spec((len(POOL_WINDOWS), B_GROUP, B_GROUP)),
            _const_spec((1, B_WIDTH)),
        ],
        out_specs=[cspec(D), lspec(D)],
        out_shape=[
            jax.ShapeDtypeStruct((BATCH * CTX, D), BF16),
            jax.ShapeDtypeStruct((BATCH * SEQ, D), BF16),
        ],
        scratch_shapes=[
            pltpu.VMEM((A_HEADS // 2, 2 * ROW_BLOCK * GRID_W, 2 * A_DH), BF16),
            pltpu.VMEM((A_HEADS // 2, 2 * ROW_BLOCK * GRID_W, CTX), F32),
            pltpu.VMEM((A_HEADS // 2, 2 * ROW_BLOCK * GRID_W, CTX), BF16),
            pltpu.VMEM((A_HEADS // 2, 2 * ROW_BLOCK * GRID_W, 2 * A_DH), F32),
            pltpu.VMEM((A_HEADS // 2, 2 * ROW_BLOCK * GRID_W, 1), F32),
        ],
        compiler_params=_params(1),
        name="ab_core",
    )(qc, kc, vc, uc, ql, kl, vl, ul, bias, pool_w, pool_scale)


HG_SMALL_LEVELS = (2, 4)


def _hg_constants():
    t = jnp.arange(HG_CHUNK)
    low = (t[:, None] >= t[None, :]).astype(BF16)
    cum = jnp.stack([jnp.concatenate([low, low], axis=1), jnp.concatenate([low.T, low.T], axis=1)])
    upper = jnp.stack([(t & c) != 0 for c in HG_SMALL_LEVELS])
    sgn_f = jnp.where(upper, 1.0, -1.0).astype(F32)
    sgn = jnp.stack([sgn_f, -sgn_f])[:, :, :, None]
    return cum, jnp.broadcast_to(sgn, (2, len(HG_SMALL_LEVELS), HG_CHUNK, HG_DK))


def _bcast_row(x3, k):
    return jnp.broadcast_to(x3[:, k:k + 1, :], x3.shape)


def _hg_scores(q, k, g, sgn_ref, level_of, reverse):
    n8, n16 = HG_CHUNK // 8, HG_CHUNK // 16
    row = lax.broadcasted_iota(jnp.int32, (HG_CHUNK, HG_CHUNK), 0)
    colm = lax.broadcasted_iota(jnp.int32, (HG_CHUNK, HG_CHUNK), 1)
    a = jnp.where(row == colm, jnp.sum(q * k, axis=-1, keepdims=True), 0.0)
    qb = q.astype(BF16)
    kb = k.astype(BF16)
    a = jnp.where(level_of == 0, _dot_nt(qb * (1.0 - k).astype(BF16), kb), a)

    g3 = g.reshape(n8, 8, HG_DK)
    sub = lax.broadcasted_iota(jnp.int32, g3.shape, 1)
    pick = (lambda r: r + 1) if reverse else (lambda r: r)
    small_refs = (jnp.where(sub < 4, _bcast_row(g3, pick(1)), _bcast_row(g3, pick(5))), _bcast_row(g3, pick(3)))
    for i, ref in enumerate(small_refs):
        w = jnp.exp2((g - ref.reshape(HG_CHUNK, HG_DK)) * sgn_ref[i]).astype(BF16)
        a = jnp.where(level_of == i + 1, _dot_nt(qb * w, kb * w), a)

    edge = _bcast_row(g3, 0 if reverse else 7)
    qb3, kb3 = qb.reshape(n16, 16, HG_DK), kb.reshape(n16, 16, HG_DK)
    lvl3 = level_of.reshape(n16, 16, HG_CHUNK)
    for lvl, c in [(l, 1 << l) for l in range(3, HG_CHUNK.bit_length() - 1)]:
        span = 2 * c // 8
        dist = []
        for j in range(n8):
            b = edge[(j // span) * span + span // 2 - (0 if reverse else 1)]
            upper = j % span >= span // 2
            dist.append(g3[j] - b if upper != reverse else b - g3[j])
        w = jnp.exp2(jnp.concatenate(dist, axis=0)).astype(BF16)
        if c == 8:
            a = jnp.where(level_of == lvl, _dot_nt(qb * w, kb * w), a)
            continue
        w3 = w.reshape(n16, 16, HG_DK)
        span16 = span // 2
        is_query = [(j % span16 >= span16 // 2) != reverse for j in range(n16)]
        lhs = jnp.concatenate([qb3[j] * w3[j] for j in range(n16) if is_query[j]], axis=0)
        rhs = jnp.concatenate([kb3[j] if is_query[j] else kb3[j] * w3[j] for j in range(n16)], axis=0)
        a_l = _dot_nt(lhs, rhs).reshape(n16 // 2, 16, HG_CHUNK)
        a3 = a.reshape(n16, 16, HG_CHUNK)
        groups, u = [], 0
        for j in range(n16):
            if is_query[j]:
                groups.append(jnp.where(lvl3[j] == lvl, a_l[u], a3[j]))
                u += 1
            else:
                groups.append(a3[j])
        a = jnp.concatenate(groups, axis=0)
    return a.astype(BF16)


def _hg_state(k, v, g, st, reverse):
    g_end = g[0:1, :] if reverse else g[HG_CHUNK - 1:HG_CHUNK, :]
    kp = (k * jnp.exp2(g_end - g)).astype(BF16)
    return st * jnp.exp2(g_end) + _dot_tn(v.astype(BF16), kp)


def _hg_out(q, v, g, st, a):
    qp = (q * jnp.exp2(g)).astype(BF16)
    return _dot_nt(qp, st.astype(BF16)) + _dot(a, v.astype(BF16))


def _hg_kernel(xc_ref, xl_ref, wq_ref, wi_ref, wff_ref, wfb_ref, wg_ref, lb_ref, gn_ref, cum_ref, sgn_ref, o_ref,
               q_s, v_s, kf_s, kb_s, gf_s, gb_s, of_s, ob_s, a_s):
    n_ctx_chunks = CTX // HG_CHUNK
    n_lat_chunks = SEQ // HG_CHUNK
    heads = range(HG_STEP_HEADS)
    dirs = ((False, kf_s, gf_s, of_s), (True, kb_s, gb_s, ob_s))

    def project(x_ref, base, n):
        x = x_ref[...]
        pq, pv, pff, pfb, pg = (_dot(x, w_ref[...]) for w_ref in (wq_ref, wi_ref, wff_ref, wfb_ref, wg_ref))
        rows = slice(base, base + n)
        gates = []
        for hh in heads:
            cols = slice(hh * HG_DK, (hh + 1) * HG_DK)
            q_s[hh, rows, :] = _silu(pq[:, cols])
            v_s[hh, rows, :] = pv[:, cols]
            for d, (_, k_s, g_s, _) in enumerate(dirs):
                lbd = lb_ref[d:d + 1, cols]
                f = lbd + (1.0 - lbd) * _sigmoid((pff, pfb)[d][:, cols])
                k_s[hh, rows, :] = 1.0 - f
                lf = jnp.log2(f)
                hi = lf.astype(BF16)
                lo = (lf - hi.astype(F32)).astype(BF16)
                for i in range(n // HG_CHUNK):
                    blk = slice(i * HG_CHUNK, (i + 1) * HG_CHUNK)
                    g_s[hh, base + i * HG_CHUNK:base + (i + 1) * HG_CHUNK, :] = _dot(
                        cum_ref[d], jnp.concatenate([hi[blk], lo[blk]], axis=0))
            gates.append(pg[:, cols])
        return gates

    project(xc_ref, 0, CTX)
    gates = project(xl_ref, CTX, SEQ)

    row = lax.broadcasted_iota(jnp.int32, (HG_CHUNK, HG_CHUNK), 0)
    colm = lax.broadcasted_iota(jnp.int32, (HG_CHUNK, HG_CHUNK), 1)
    diff = row ^ colm
    lvl = jnp.zeros_like(diff)
    for c in range(1, HG_CHUNK.bit_length() - 1):
        lvl = lvl + (diff >= (1 << c)).astype(jnp.int32)
    lvl_of = (jnp.where(row > colm, lvl, -1), jnp.where(row < colm, lvl, -1))

    def operands(hh, d, base):
        rows = pl.ds(base, HG_CHUNK)
        _, k_s, g_s, _ = dirs[d]
        return q_s[hh, rows, :], k_s[hh, rows, :], v_s[hh, rows, :], g_s[hh, rows, :]

    sts = [jnp.zeros((HG_DK, HG_DK), F32) for _ in heads for _ in dirs]
    for i in range(n_ctx_chunks):
        for hh in heads:
            for d, (reverse, *_) in enumerate(dirs):
                j = (n_ctx_chunks - 1 - i) if reverse else i
                _, k, v, g = operands(hh, d, j * HG_CHUNK)
                sts[2 * hh + d] = _hg_state(k, v, g, sts[2 * hh + d], reverse)

    def lat_base(i, reverse):
        j = (n_lat_chunks - 1 - i) if reverse else i
        return pl.multiple_of(j * HG_CHUNK, HG_CHUNK)

    def scores(i):
        for hh in heads:
            for d, (reverse, *_) in enumerate(dirs):
                q, k, _, g = operands(hh, d, CTX + lat_base(i, reverse))
                a_s[hh, d, i] = _hg_scores(q, k, g, sgn_ref.at[d], lvl_of[d], reverse)

    def body(i, carry):
        sts = list(carry)
        for hh in heads:
            for d, (reverse, _, _, o_s) in enumerate(dirs):
                base = lat_base(i, reverse)
                q, k, v, g = operands(hh, d, CTX + base)
                o_s[hh, pl.ds(base, HG_CHUNK), :] = _hg_out(q, v, g, sts[2 * hh + d], a_s[hh, d, i])
                sts[2 * hh + d] = _hg_state(k, v, g, sts[2 * hh + d], reverse)
        scores(jnp.minimum(i + 1, n_lat_chunks - 1))
        return tuple(sts)

    scores(0)
    lax.fori_loop(0, n_lat_chunks, body, tuple(sts), unroll=2)

    for hh in heads:
        o = of_s[hh] + ob_s[hh]
        o = o * lax.rsqrt(jnp.mean(o * o, axis=-1, keepdims=True) + EPS) * gn_ref[...]
        o_ref[:, hh * HG_DK:(hh + 1) * HG_DK] = (o * _silu(gates[hh])).astype(o_ref.dtype)


def _hg_core(xn_ctx, xn_lat, w_in, lb, gnorm):
    nh = HG_STEP_HEADS
    steps = HG_HEADS // nh
    seq_scr = pltpu.VMEM((nh, CTX + SEQ, HG_DK), F32)
    lat_scr = pltpu.VMEM((nh, SEQ, HG_DK), F32)
    a_scr = pltpu.VMEM((nh, 2, SEQ // HG_CHUNK, HG_CHUNK, HG_CHUNK), BF16)
    cum, sgn = _hg_constants()
    return pl.pallas_call(
        _hg_kernel,
        grid=(BATCH, steps),
        in_specs=[
            pl.BlockSpec((CTX, D), lambda b, h: (b, 0)),
            pl.BlockSpec((SEQ, D), lambda b, h: (b, 0)),
            *[pl.BlockSpec((D, nh * HG_DK), lambda b, h, j=j: (0, j * steps + h)) for j in range(5)],
            pl.BlockSpec((2, nh * HG_DK), lambda b, h: (0, h)),
            pl.BlockSpec((1, HG_DK), lambda b, h: (0, 0)),
            _const_spec(cum.shape),
            _const_spec(sgn.shape),
        ],
        out_specs=pl.BlockSpec((SEQ, nh * HG_DK), lambda b, h: (b, h)),
        out_shape=jax.ShapeDtypeStruct((BATCH * SEQ, HG_KDIM), BF16),
        scratch_shapes=[seq_scr] * 6 + [lat_scr] * 2 + [a_scr],
        compiler_params=_params(2),
        name="hgrn2",
    )(xn_ctx, xn_lat, *([w_in] * 5), lb, gnorm, cum, sgn)


def kernel(x, c, ctx, c_ctx, w_mod, b_mod, norm_g, ffn_w13, ffn_w2, ab_w_in, ab_rpb, ab_pool_w, ab_pool_scale,
           ab_w_out, hg_w_in, hg_lb_logits, hg_gnorm, hg_w_out, final_g):
    h_lat = x.reshape(BATCH * SEQ, D)
    h_ctx = ctx.reshape(BATCH * CTX, D)

    cvec = jnp.zeros((MOD_ROWS, D), F32).at[0].set(c_ctx).at[1:1 + BATCH].set(c)
    mod = _modulation(cvec, w_mod, b_mod)

    w13 = ffn_w13.astype(BF16)
    w2 = ffn_w2.astype(BF16)
    ffn = functools.partial(_rows, mod=mod, norm_g=norm_g, w13=w13, w2=w2)

    w_in = ab_w_in[0].astype(BF16)
    h_lat, ql, kl, vl, ul = ffn(h_lat, layer=0, sub=0, is_ctx=False, post="ab", w_post=w_in)
    h_ctx, qc, kc, vc, uc = ffn(h_ctx, layer=0, sub=0, is_ctx=True, post="ab", w_post=w_in)
    mix_ctx, mix_lat = _ab_core(qc, kc, vc, uc, ql, kl, vl, ul, _attn_bias_table(ab_rpb[0]),
                                ab_pool_w[0].astype(BF16), ab_pool_scale[0].reshape(1, B_WIDTH))
    w_out = ab_w_out[0].astype(BF16)
    (h_lat,) = ffn(h_lat, layer=0, sub=2, is_ctx=False, mix=(mix_lat, w_out))
    (h_ctx,) = ffn(h_ctx, layer=0, sub=2, is_ctx=True, mix=(mix_ctx, w_out))

    h_lat, xn_lat = ffn(h_lat, layer=1, sub=0, is_ctx=False, post="xn")
    _, xn_ctx = ffn(h_ctx, layer=1, sub=0, is_ctx=True, post="xn")
    sm = jax.nn.softmax(hg_lb_logits.astype(F32), axis=0)
    lb = (jnp.cumsum(sm, axis=0) - sm[0:1])[1]
    o = _hg_core(xn_ctx, xn_lat, hg_w_in[0].astype(BF16), lb, hg_gnorm[0].reshape(1, HG_DK))
    (out,) = ffn(h_lat, layer=1, sub=2, is_ctx=False, mix=(o, hg_w_out[0].astype(BF16)), post="final",
                 w_post=final_g)
    return out.reshape(BATCH, SEQ, D)
```

```python
import functools

import jax
import jax.numpy as jnp
from jax import lax
from jax.experimental import pallas as pl
from jax.experimental.pallas import tpu as pltpu

D = 1024
BATCH = 16
SEQ = 2048
DEPTH = 2
GRID_W = 64
CTX = 256
EPS = 1e-6
N_MOD = 9
D_FF = 2816
A_HEADS = 8
A_DH = 64
A_WIDTH = 512
WIN_R = 8
WIN_C = 16
POOL_WINDOWS = (2, 4, 8, 16)
B_WIDTH = 512
B_GROUP = 128
HG_HEADS = 8
HG_DK = 128
HG_KDIM = 1024

ROWS = SEQ // GRID_W
ROW_BLOCK = 8
MOD_ROWS = 24
ROW_TILE = 1024
ROW_TILE_AB = 512
FF_CHUNK = 256
HG_CHUNK = 128
HG_STEP_HEADS = 2
NEG_BIG = -1e30
VMEM_LIMIT = 56 * 1024 * 1024

F32 = jnp.float32
BF16 = jnp.bfloat16


def _dot(a, b):
    return jnp.dot(a, b, preferred_element_type=F32)


def _dot_nt(a, b):
    return lax.dot_general(a, b, (((1,), (1,)), ((), ())), preferred_element_type=F32)


def _dot_tn(a, b):
    return lax.dot_general(a, b, (((0,), (0,)), ((), ())), preferred_element_type=F32)


def _sigmoid(x):
    return 1.0 / (1.0 + jnp.exp(-x))


def _silu(x):
    return x * _sigmoid(x)


def _params(n_axes):
    return pltpu.CompilerParams(dimension_semantics=("arbitrary",) * n_axes, vmem_limit_bytes=VMEM_LIMIT)


def _const_spec(shape):
    nd = len(shape)
    return pl.BlockSpec(shape, lambda *_: (0,) * nd, pipeline_mode=pl.Buffered(1))


def _adaln(x, g, shift, scale):
    ms = jnp.mean(x * x, axis=-1, keepdims=True)
    return (x * lax.rsqrt(ms + EPS)) * (g * (1.0 + scale)) + shift


def _mod_kernel(c_ref, w_ref, b_ref, o_ref):
    s = _silu(c_ref[...]).astype(BF16)
    o_ref[...] = _dot(s, w_ref[...].astype(BF16)) + b_ref[...]


def _modulation(cvec, w_mod, b_mod):
    nblk = N_MOD
    out = pl.pallas_call(
        _mod_kernel,
        grid=(DEPTH, nblk),
        in_specs=[
            pl.BlockSpec((MOD_ROWS, D), lambda l, j: (0, 0)),
            pl.BlockSpec((None, D, D), lambda l, j: (l, 0, j)),
            pl.BlockSpec((None, 1, D), lambda l, j: (l, 0, j)),
        ],
        out_specs=pl.BlockSpec((None, MOD_ROWS, D), lambda l, j: (l, 0, j)),
        out_shape=jax.ShapeDtypeStruct((DEPTH, MOD_ROWS, N_MOD * D), F32),
        compiler_params=_params(2),
        name="modulation",
    )(cvec, w_mod, b_mod.reshape(DEPTH, 1, N_MOD * D))
    return out.reshape(DEPTH, MOD_ROWS, N_MOD, D)


def _mod_spec(layer, is_ctx, tile):
    tiles_per_batch = SEQ // tile
    if is_ctx:
        return pl.BlockSpec((None, None, N_MOD, D), lambda t: (layer, 0, 0, 0))
    return pl.BlockSpec((None, None, N_MOD, D), lambda t: (layer, 1 + t // tiles_per_batch, 0, 0))


def _norm_spec(layer):
    return pl.BlockSpec((None, 3, D), lambda t: (layer, 0, 0))


def _row_spec(width, tile):
    return pl.BlockSpec((tile, width), lambda t: (t, 0))


def _rows_kernel(sub, pre_mix, post, *refs):
    refs = list(refs)
    h_ref, mod_ref, g_ref = refs[:3]
    del refs[:3]
    if pre_mix:
        x_ref, wmix_ref = refs[:2]
        del refs[:2]
    w13_ref, w2_ref = refs[:2]
    del refs[:2]
    if post in ("ab", "final"):
        wpost_ref = refs.pop(0)
    o_ref = refs.pop(0)

    h = h_ref[...]
    if pre_mix:
        h = h + mod_ref[5:6, :] * _dot(x_ref[...], wmix_ref[...])
    shift = mod_ref[3 * sub:3 * sub + 1, :]
    scale = mod_ref[3 * sub + 1:3 * sub + 2, :]
    gate = mod_ref[3 * sub + 2:3 * sub + 3, :]
    xb = _adaln(h, g_ref[sub:sub + 1, :], shift, scale).astype(BF16)
    acc = jnp.zeros(h.shape, F32)
    for c in range(D_FF // FF_CHUNK):
        lo = c * FF_CHUNK
        a = _dot(xb, w13_ref[:, lo:lo + FF_CHUNK])
        b = _dot(xb, w13_ref[:, D_FF + lo:D_FF + lo + FF_CHUNK])
        act = (_silu(a) * b).astype(BF16)
        acc = acc + _dot(act, w2_ref[lo:lo + FF_CHUNK, :])
    y = h + (0.5 * gate) * acc
    if post == "final":
        ms = jnp.mean(y * y, axis=-1, keepdims=True)
        y = (y * lax.rsqrt(ms + EPS)) * wpost_ref[...]
    o_ref[...] = y
    if post in ("ab", "xn"):
        xm = _adaln(y, g_ref[1:2, :], mod_ref[3:4, :], mod_ref[4:5, :]).astype(BF16)
    if post == "xn":
        refs[0][...] = xm
    if post == "ab":
        q_ref, k_ref, v_ref, u_ref = refs
        q_ref[...] = (_dot(xm, wpost_ref[:, 0:A_WIDTH]) * (A_DH ** -0.5)).astype(BF16)
        k_ref[...] = _dot(xm, wpost_ref[:, A_WIDTH:2 * A_WIDTH]).astype(BF16)
        v_ref[...] = _dot(xm, wpost_ref[:, 2 * A_WIDTH:3 * A_WIDTH]).astype(BF16)
        u_ref[...] = _dot(xm, wpost_ref[:, 3 * A_WIDTH:])


def _rows(h, mod, norm_g, w13, w2, layer, sub, is_ctx, mix=None, post=None, w_post=None):
    n = h.shape[0]
    which = sub // 2
    tile = ROW_TILE_AB if post == "ab" else ROW_TILE
    row_spec = functools.partial(_row_spec, tile=tile)
    in_specs = [row_spec(D), _mod_spec(layer, is_ctx, tile), _norm_spec(layer)]
    args = [h, mod, norm_g]
    if mix is not None:
        in_specs += [row_spec(D), _const_spec((D, D))]
        args += list(mix)
    in_specs += [
        pl.BlockSpec((None, None, D, 2 * D_FF), lambda t: (layer, which, 0, 0), pipeline_mode=pl.Buffered(1)),
        pl.BlockSpec((None, None, D_FF, D), lambda t: (layer, which, 0, 0), pipeline_mode=pl.Buffered(1)),
    ]
    args += [w13, w2]
    out_specs = [row_spec(D)]
    out_shape = [jax.ShapeDtypeStruct((n, D), F32)]
    if post == "final":
        in_specs.append(_const_spec((1, D)))
        args.append(w_post.reshape(1, D))
    elif post == "ab":
        in_specs.append(_const_spec((D, 3 * A_WIDTH + B_WIDTH)))
        args.append(w_post)
        out_specs += [row_spec(A_WIDTH)] * 3 + [row_spec(B_WIDTH)]
        out_shape += [jax.ShapeDtypeStruct((n, A_WIDTH), BF16)] * 3 + [jax.ShapeDtypeStruct((n, B_WIDTH), F32)]
    elif post == "xn":
        out_specs.append(row_spec(D))
        out_shape.append(jax.ShapeDtypeStruct((n, D), BF16))
    return pl.pallas_call(
        functools.partial(_rows_kernel, sub, mix is not None, post),
        grid=(n // tile,),
        in_specs=in_specs,
        out_specs=out_specs,
        out_shape=out_shape,
        compiler_params=_params(1),
        name="rows",
    )(*args)


def _attn_bias_table(rpb):
    col = jnp.arange(GRID_W)
    win_c0 = jnp.clip(col - WIN_C // 2, 0, GRID_W - WIN_C)
    kc = jnp.arange(GRID_W)
    col_ok = (kc[None, :] >= win_c0[:, None]) & (kc[None, :] < win_c0[:, None] + WIN_C)
    rel_c = jnp.clip(kc[None, :] - col[:, None], -(WIN_C - 1), WIN_C - 1) + (WIN_C - 1)
    onehot = (rel_c[None] == jnp.arange(2 * WIN_C - 1)[:, None, None]).astype(F32)
    by_col = jnp.einsum("hri,iqk->hqrk", rpb.astype(F32), onehot, precision=lax.Precision.HIGHEST)
    by_col = jnp.where(col_ok[None, :, None, :], by_col, NEG_BIG)
    tab = jnp.stack([by_col[:, :, WIN_R - 1 - dr:2 * WIN_R - 1 - dr] for dr in range(WIN_R)])
    return tab.reshape(WIN_R, A_HEADS * GRID_W, WIN_R * GRID_W)


def _softmax_parts(s_parts):
    m = s_parts[0].max(axis=-1, keepdims=True)
    for s in s_parts[1:]:
        m = jnp.maximum(m, s.max(axis=-1, keepdims=True))
    ps = [jnp.exp(s - m) for s in s_parts]
    den = ps[0].sum(axis=-1, keepdims=True)
    for p in ps[1:]:
        den = den + p.sum(axis=-1, keepdims=True)
    return [p.astype(BF16) for p in ps], 1.0 / den


def _stack_heads(x2, first):
    zero = jnp.zeros((), x2.dtype)
    return jnp.concatenate([jnp.where(first, x2, zero), jnp.where(first, zero, x2)], axis=0)


def _unstack_heads(o, first):
    m = o.shape[0] // 2
    return jnp.where(first, o[:m], o[m:])


def _shift_rows(x, k, row, n):
    if k > 0:
        return jnp.where(row >= k, pltpu.roll(x, k, 0), 0.0)
    return jnp.where(row < n + k, pltpu.roll(x, n + k, 0), 0.0)


def _pool_mix(u_ref, pw_ref, ps_ref, o_ref, n):
    row = lax.broadcasted_iota(jnp.int32, (n, B_GROUP), 0)
    rowf = row.astype(F32)
    for gi, w in enumerate(POOL_WINDOWS):
        x = u_ref[:, gi * B_GROUP:(gi + 1) * B_GROUP]
        half = w // 2
        back = x
        fwd = x
        m = 1
        while m < half:
            back = back + _shift_rows(back, m, row, n)
            fwd = fwd + _shift_rows(fwd, -m, row, n)
            m *= 2
        total = _shift_rows(back, 1, row, n) + fwd
        cnt = jnp.minimum(rowf + half, float(n)) - jnp.maximum(rowf - half, 0.0)
        d = (total / cnt - x).astype(BF16)
        y = _dot(d, pw_ref[gi]) * ps_ref[:, gi * B_GROUP:(gi + 1) * B_GROUP]
        o_ref[:, A_WIDTH + gi * B_GROUP:A_WIDTH + (gi + 1) * B_GROUP] = y.astype(o_ref.dtype)


def _ab_core_kernel(qc_ref, kc_ref, vc_ref, uc_ref, ql_ref, kl_ref, vl_ref, ul_ref, bias_ref, pw_ref, ps_ref,
                    oc_ref, ol_ref, qs_s, sc_s, pc_s, ol_s, inv_s):
    lane = lax.broadcasted_iota(jnp.int32, (1, 2 * A_DH), 1)
    first = lane < A_DH
    pairs = [slice(p * 2 * A_DH, (p + 1) * 2 * A_DH) for p in range(A_HEADS // 2)]

    qs = [_stack_heads(qc_ref[:, cols], first) for cols in pairs]
    s_all = jnp.concatenate([_dot_nt(q, kc_ref[:, cols]) for q, cols in zip(qs, pairs)], axis=0)
    (p_all,), inv = _softmax_parts([s_all])
    for i, cols in enumerate(pairs):
        rows = slice(i * 2 * CTX, (i + 1) * 2 * CTX)
        o = _dot(p_all[rows], vc_ref[:, cols]) * inv[rows]
        oc_ref[:, cols] = _unstack_heads(o, first).astype(oc_ref.dtype)

    n_pairs = len(pairs)
    blk_q = ROW_BLOCK * GRID_W
    per_row = 2 * GRID_W

    def block_body(blk, carry):
        q0 = pl.multiple_of(blk * blk_q, blk_q)
        for i, cols in enumerate(pairs):
            for j in range(ROW_BLOCK):
                q = _stack_heads(ql_ref[pl.ds(q0 + j * GRID_W, GRID_W), cols], first)
                qs_s[i, j * per_row:(j + 1) * per_row, :] = q
            sc_s[i] = _dot_nt(qs_s[i], kc_ref[:, cols])

        def row_body(j, carry):
            r = blk * ROW_BLOCK + j
            r0 = jnp.clip(r - WIN_R // 2, 0, ROWS - WIN_R)
            krows = pl.ds(pl.multiple_of(r0 * GRID_W, GRID_W), WIN_R * GRID_W)
            rows_j = pl.ds(pl.multiple_of(j * per_row, per_row), per_row)
            s_loc = jnp.concatenate([_dot_nt(qs_s[i, rows_j, :], kl_ref[krows, cols])
                                     for i, cols in enumerate(pairs)], axis=0)
            s_ctx = jnp.concatenate([sc_s[i, rows_j, :] for i in range(n_pairs)], axis=0)
            (p_loc, p_ctx), inv = _softmax_parts([s_loc + bias_ref[r - r0], s_ctx])
            for i, cols in enumerate(pairs):
                rows = slice(i * per_row, (i + 1) * per_row)
                pc_s[i, rows_j, :] = p_ctx[rows]
                ol_s[i, rows_j, :] = _dot(p_loc[rows], vl_ref[krows, cols])
                inv_s[i, rows_j, :] = inv[rows]
            return carry

        lax.fori_loop(0, ROW_BLOCK, row_body, 0, unroll=4)

        for i, cols in enumerate(pairs):
            o = (ol_s[i] + _dot(pc_s[i], vc_ref[:, cols])) * inv_s[i]
            for j in range(ROW_BLOCK):
                ol_ref[pl.ds(q0 + j * GRID_W, GRID_W), cols] = _unstack_heads(
                    o[j * per_row:(j + 1) * per_row], first).astype(ol_ref.dtype)
        return carry

    lax.fori_loop(0, ROWS // ROW_BLOCK, block_body, 0)

    _pool_mix(uc_ref, pw_ref, ps_ref, oc_ref, CTX)
    _pool_mix(ul_ref, pw_ref, ps_ref, ol_ref, SEQ)


def _ab_core(qc, kc, vc, uc, ql, kl, vl, ul, bias, pool_w, pool_scale):
    cspec = lambda w: pl.BlockSpec((CTX, w), lambda b: (b, 0))
    lspec = lambda w: pl.BlockSpec((SEQ, w), lambda b: (b, 0))
    return pl.pallas_call(
        _ab_core_kernel,
        grid=(BATCH,),
        in_specs=[
            cspec(A_WIDTH), cspec(A_WIDTH), cspec(A_WIDTH), cspec(B_WIDTH),
            lspec(A_WIDTH), lspec(A_WIDTH), lspec(A_WIDTH), lspec(B_WIDTH),
            _const_spec((WIN_R, A_HEADS * GRID_W, WIN_R * GRID_W)),
            _const_spec((len(POOL_WINDOWS), B_GROUP, B_GROUP)),
            _const_spec((1, B_WIDTH)),
        ],
        out_specs=[cspec(D), lspec(D)],
        out_shape=[
            jax.ShapeDtypeStruct((BATCH * CTX, D), BF16),
            jax.ShapeDtypeStruct((BATCH * SEQ, D), BF16),
        ],
        scratch_shapes=[
            pltpu.VMEM((A_HEADS // 2, 2 * ROW_BLOCK * GRID_W, 2 * A_DH), BF16),
            pltpu.VMEM((A_HEADS // 2, 2 * ROW_BLOCK * GRID_W, CTX), F32),
            pltpu.VMEM((A_HEADS // 2, 2 * ROW_BLOCK * GRID_W, CTX), BF16),
            pltpu.VMEM((A_HEADS // 2, 2 * ROW_BLOCK * GRID_W, 2 * A_DH), F32),
            pltpu.VMEM((A_HEADS // 2, 2 * ROW_BLOCK * GRID_W, 1), F32),
        ],
        compiler_params=_params(1),
        name="ab_core",
    )(qc, kc, vc, uc, ql, kl, vl, ul, bias, pool_w, pool_scale)


HG_SMALL_LEVELS = (2, 4)


def _hg_constants():
    t = jnp.arange(HG_CHUNK)
    low = (t[:, None] >= t[None, :]).astype(BF16)
    cum = jnp.stack([jnp.concatenate([low, low], axis=1), jnp.concatenate([low.T, low.T], axis=1)])
    upper = jnp.stack([(t & c) != 0 for c in HG_SMALL_LEVELS])
    sgn_f = jnp.where(upper, 1.0, -1.0).astype(F32)
    sgn = jnp.stack([sgn_f, -sgn_f])[:, :, :, None]
    return cum, jnp.broadcast_to(sgn, (2, len(HG_SMALL_LEVELS), HG_CHUNK, HG_DK))


def _bcast_row(x3, k):
    return jnp.broadcast_to(x3[:, k:k + 1, :], x3.shape)


def _hg_scores(q, k, g, sgn_ref, level_of, reverse):
    n8, n16 = HG_CHUNK // 8, HG_CHUNK // 16
    row = lax.broadcasted_iota(jnp.int32, (HG_CHUNK, HG_CHUNK), 0)
    colm = lax.broadcasted_iota(jnp.int32, (HG_CHUNK, HG_CHUNK), 1)
    a = jnp.where(row == colm, jnp.sum(q * k, axis=-1, keepdims=True), 0.0)
    qb = q.astype(BF16)
    kb = k.astype(BF16)
    a = jnp.where(level_of == 0, _dot_nt(qb * (1.0 - k).astype(BF16), kb), a)

    g3 = g.reshape(n8, 8, HG_DK)
    sub = lax.broadcasted_iota(jnp.int32, g3.shape, 1)
    pick = (lambda r: r + 1) if reverse else (lambda r: r)
    small_refs = (jnp.where(sub < 4, _bcast_row(g3, pick(1)), _bcast_row(g3, pick(5))), _bcast_row(g3, pick(3)))
    for i, ref in enumerate(small_refs):
        w = jnp.exp2((g - ref.reshape(HG_CHUNK, HG_DK)) * sgn_ref[i]).astype(BF16)
        a = jnp.where(level_of == i + 1, _dot_nt(qb * w, kb * w), a)

    edge = _bcast_row(g3, 0 if reverse else 7)
    qb3, kb3 = qb.reshape(n16, 16, HG_DK), kb.reshape(n16, 16, HG_DK)
    lvl3 = level_of.reshape(n16, 16, HG_CHUNK)
    for lvl, c in [(l, 1 << l) for l in range(3, HG_CHUNK.bit_length() - 1)]:
        span = 2 * c // 8
        dist = []
        for j in range(n8):
            b = edge[(j // span) * span + span // 2 - (0 if reverse else 1)]
            upper = j % span >= span // 2
            dist.append(g3[j] - b if upper != reverse else b - g3[j])
        w = jnp.exp2(jnp.concatenate(dist, axis=0)).astype(BF16)
        if c == 8:
            a = jnp.where(level_of == lvl, _dot_nt(qb * w, kb * w), a)
            continue
        w3 = w.reshape(n16, 16, HG_DK)
        span16 = span // 2
        is_query = [(j % span16 >= span16 // 2) != reverse for j in range(n16)]
        lhs = jnp.concatenate([qb3[j] * w3[j] for j in range(n16) if is_query[j]], axis=0)
        rhs = jnp.concatenate([kb3[j] if is_query[j] else kb3[j] * w3[j] for j in range(n16)], axis=0)
        a_l = _dot_nt(lhs, rhs).reshape(n16 // 2, 16, HG_CHUNK)
        a3 = a.reshape(n16, 16, HG_CHUNK)
        groups, u = [], 0
        for j in range(n16):
            if is_query[j]:
                groups.append(jnp.where(lvl3[j] == lvl, a_l[u], a3[j]))
                u += 1
            else:
                groups.append(a3[j])
        a = jnp.concatenate(groups, axis=0)
    return a.astype(BF16)


def _hg_state(k, v, g, st, reverse):
    g_end = g[0:1, :] if reverse else g[HG_CHUNK - 1:HG_CHUNK, :]
    kp = (k * jnp.exp2(g_end - g)).astype(BF16)
    return st * jnp.exp2(g_end) + _dot_tn(v.astype(BF16), kp)


def _hg_out(q, v, g, st, a):
    qp = (q * jnp.exp2(g)).astype(BF16)
    return _dot_nt(qp, st.astype(BF16)) + _dot(a, v.astype(BF16))


def _hg_kernel(xc_ref, xl_ref, wq_ref, wi_ref, wff_ref, wfb_ref, wg_ref, lb_ref, gn_ref, cum_ref, sgn_ref, o_ref,
               q_s, v_s, kf_s, kb_s, gf_s, gb_s, of_s, ob_s, a_s):
    n_ctx_chunks = CTX // HG_CHUNK
    n_lat_chunks = SEQ // HG_CHUNK
    heads = range(HG_STEP_HEADS)
    dirs = ((False, kf_s, gf_s, of_s), (True, kb_s, gb_s, ob_s))

    def project(x_ref, base, n):
        x = x_ref[...]
        pq, pv, pff, pfb, pg = (_dot(x, w_ref[...]) for w_ref in (wq_ref, wi_ref, wff_ref, wfb_ref, wg_ref))
        rows = slice(base, base + n)
        gates = []
        for hh in heads:
            cols = slice(hh * HG_DK, (hh + 1) * HG_DK)
            q_s[hh, rows, :] = _silu(pq[:, cols])
            v_s[hh, rows, :] = pv[:, cols]
            for d, (_, k_s, g_s, _) in enumerate(dirs):
                lbd = lb_ref[d:d + 1, cols]
                f = lbd + (1.0 - lbd) * _sigmoid((pff, pfb)[d][:, cols])
                k_s[hh, rows, :] = 1.0 - f
                lf = jnp.log2(f)
                hi = lf.astype(BF16)
                lo = (lf - hi.astype(F32)).astype(BF16)
                for i in range(n // HG_CHUNK):
                    blk = slice(i * HG_CHUNK, (i + 1) * HG_CHUNK)
                    g_s[hh, base + i * HG_CHUNK:base + (i + 1) * HG_CHUNK, :] = _dot(
                        cum_ref[d], jnp.concatenate([hi[blk], lo[blk]], axis=0))
            gates.append(pg[:, cols])
        return gates

    project(xc_ref, 0, CTX)
    gates = project(xl_ref, CTX, SEQ)

    row = lax.broadcasted_iota(jnp.int32, (HG_CHUNK, HG_CHUNK), 0)
    colm = lax.broadcasted_iota(jnp.int32, (HG_CHUNK, HG_CHUNK), 1)
    diff = row ^ colm
    lvl = jnp.zeros_like(diff)
    for c in range(1, HG_CHUNK.bit_length() - 1):
        lvl = lvl + (diff >= (1 << c)).astype(jnp.int32)
    lvl_of = (jnp.where(row > colm, lvl, -1), jnp.where(row < colm, lvl, -1))

    def operands(hh, d, base):
        rows = pl.ds(base, HG_CHUNK)
        _, k_s, g_s, _ = dirs[d]
        return q_s[hh, rows, :], k_s[hh, rows, :], v_s[hh, rows, :], g_s[hh, rows, :]

    sts = [jnp.zeros((HG_DK, HG_DK), F32) for _ in heads for _ in dirs]
    for i in range(n_ctx_chunks):
        for hh in heads:
            for d, (reverse, *_) in enumerate(dirs):
                j = (n_ctx_chunks - 1 - i) if reverse else i
                _, k, v, g = operands(hh, d, j * HG_CHUNK)
                sts[2 * hh + d] = _hg_state(k, v, g, sts[2 * hh + d], reverse)

    def lat_base(i, reverse):
        j = (n_lat_chunks - 1 - i) if reverse else i
        return pl.multiple_of(j * HG_CHUNK, HG_CHUNK)

    def scores(i):
        for hh in heads:
            for d, (reverse, *_) in enumerate(dirs):
                q, k, _, g = operands(hh, d, CTX + lat_base(i, reverse))
                a_s[hh, d, i] = _hg_scores(q, k, g, sgn_ref.at[d], lvl_of[d], reverse)

    def body(i, carry):
        sts = list(carry)
        for hh in heads:
            for d, (reverse, _, _, o_s) in enumerate(dirs):
                base = lat_base(i, reverse)
                q, k, v, g = operands(hh, d, CTX + base)
                o_s[hh, pl.ds(base, HG_CHUNK), :] = _hg_out(q, v, g, sts[2 * hh + d], a_s[hh, d, i])
                sts[2 * hh + d] = _hg_state(k, v, g, sts[2 * hh + d], reverse)
        scores(jnp.minimum(i + 1, n_lat_chunks - 1))
        return tuple(sts)

    scores(0)
    lax.fori_loop(0, n_lat_chunks, body, tuple(sts), unroll=8)

    for hh in heads:
        o = of_s[hh] + ob_s[hh]
        o = o * lax.rsqrt(jnp.mean(o * o, axis=-1, keepdims=True) + EPS) * gn_ref[...]
        o_ref[:, hh * HG_DK:(hh + 1) * HG_DK] = (o * _silu(gates[hh])).astype(o_ref.dtype)


def _hg_core(xn_ctx, xn_lat, w_in, lb, gnorm):
    nh = HG_STEP_HEADS
    steps = HG_HEADS // nh
    seq_scr = pltpu.VMEM((nh, CTX + SEQ, HG_DK), F32)
    lat_scr = pltpu.VMEM((nh, SEQ, HG_DK), F32)
    a_scr = pltpu.VMEM((nh, 2, SEQ // HG_CHUNK, HG_CHUNK, HG_CHUNK), BF16)
    cum, sgn = _hg_constants()
    return pl.pallas_call(
        _hg_kernel,
        grid=(BATCH, steps),
        in_specs=[
            pl.BlockSpec((CTX, D), lambda b, h: (b, 0)),
            pl.BlockSpec((SEQ, D), lambda b, h: (b, 0)),
            *[pl.BlockSpec((D, nh * HG_DK), lambda b, h, j=j: (0, j * steps + h)) for j in range(5)],
            pl.BlockSpec((2, nh * HG_DK), lambda b, h: (0, h)),
            pl.BlockSpec((1, HG_DK), lambda b, h: (0, 0)),
            _const_spec(cum.shape),
            _const_spec(sgn.shape),
        ],
        out_specs=pl.BlockSpec((SEQ, nh * HG_DK), lambda b, h: (b, h)),
        out_shape=jax.ShapeDtypeStruct((BATCH * SEQ, HG_KDIM), BF16),
        scratch_shapes=[seq_scr] * 6 + [lat_scr] * 2 + [a_scr],
        compiler_params=_params(2),
        name="hgrn2",
    )(xn_ctx, xn_lat, *([w_in] * 5), lb, gnorm, cum, sgn)


def kernel(x, c, ctx, c_ctx, w_mod, b_mod, norm_g, ffn_w13, ffn_w2, ab_w_in, ab_rpb, ab_pool_w, ab_pool_scale,
           ab_w_out, hg_w_in, hg_lb_logits, hg_gnorm, hg_w_out, final_g):
    h_lat = x.reshape(BATCH * SEQ, D)
    h_ctx = ctx.reshape(BATCH * CTX, D)

    cvec = jnp.zeros((MOD_ROWS, D), F32).at[0].set(c_ctx).at[1:1 + BATCH].set(c)
    mod = _modulation(cvec, w_mod, b_mod)

    w13 = ffn_w13.astype(BF16)
    w2 = ffn_w2.astype(BF16)
    ffn = functools.partial(_rows, mod=mod, norm_g=norm_g, w13=w13, w2=w2)

    w_in = ab_w_in[0].astype(BF16)
    h_lat, ql, kl, vl, ul = ffn(h_lat, layer=0, sub=0, is_ctx=False, post="ab", w_post=w_in)
    h_ctx, qc, kc, vc, uc = ffn(h_ctx, layer=0, sub=0, is_ctx=True, post="ab", w_post=w_in)
    mix_ctx, mix_lat = _ab_core(qc, kc, vc, uc, ql, kl, vl, ul, _attn_bias_table(ab_rpb[0]),
                                ab_pool_w[0].astype(BF16), ab_pool_scale[0].reshape(1, B_WIDTH))
    w_out = ab_w_out[0].astype(BF16)
    (h_lat,) = ffn(h_lat, layer=0, sub=2, is_ctx=False, mix=(mix_lat, w_out))
    (h_ctx,) = ffn(h_ctx, layer=0, sub=2, is_ctx=True, mix=(mix_ctx, w_out))

    h_lat, xn_lat = ffn(h_lat, layer=1, sub=0, is_ctx=False, post="xn")
    _, xn_ctx = ffn(h_ctx, layer=1, sub=0, is_ctx=True, post="xn")
    sm = jax.nn.softmax(hg_lb_logits.astype(F32), axis=0)
    lb = (jnp.cumsum(sm, axis=0) - sm[0:1])[1]
    o = _hg_core(xn_ctx, xn_lat, hg_w_in[0].astype(BF16), lb, hg_gnorm[0].reshape(1, HG_DK))
    (out,) = ffn(h_lat, layer=1, sub=2, is_ctx=False, mix=(o, hg_w_out[0].astype(BF16)), post="final",
                 w_post=final_g)
    return out.reshape(BATCH, SEQ, D)
```

```python
import functools

import jax
import jax.numpy as jnp
from jax import lax
from jax.experimental import pallas as pl
from jax.experimental.pallas import tpu as pltpu

D = 1024
BATCH = 16
SEQ = 2048
DEPTH = 2
GRID_W = 64
CTX = 256
EPS = 1e-6
N_MOD = 9
D_FF = 2816
A_HEADS = 8
A_DH = 64
A_WIDTH = 512
WIN_R = 8
WIN_C = 16
POOL_WINDOWS = (2, 4, 8, 16)
B_WIDTH = 512
B_GROUP = 128
HG_HEADS = 8
HG_DK = 128
HG_KDIM = 1024

ROWS = SEQ // GRID_W
ROW_BLOCK = 8
MOD_ROWS = 24
ROW_TILE = 1024
ROW_TILE_AB = 512
FF_CHUNK = 256
HG_CHUNK = 128
HG_STEP_HEADS = 2
NEG_BIG = -1e30
VMEM_LIMIT = 56 * 1024 * 1024

F32 = jnp.float32
BF16 = jnp.bfloat16


def _dot(a, b):
    return jnp.dot(a, b, preferred_element_type=F32)


def _dot_nt(a, b):
    return lax.dot_general(a, b, (((1,), (1,)), ((), ())), preferred_element_type=F32)


def _dot_tn(a, b):
    return lax.dot_general(a, b, (((0,), (0,)), ((), ())), preferred_element_type=F32)


def _sigmoid(x):
    return 0.5 * jnp.tanh(0.5 * x) + 0.5


def _silu(x):
    return x * _sigmoid(x)


def _params(n_axes):
    return pltpu.CompilerParams(dimension_semantics=("arbitrary",) * n_axes, vmem_limit_bytes=VMEM_LIMIT)


def _const_spec(shape):
    nd = len(shape)
    return pl.BlockSpec(shape, lambda *_: (0,) * nd, pipeline_mode=pl.Buffered(1))


def _adaln(x, g, shift, scale):
    ms = jnp.mean(x * x, axis=-1, keepdims=True)
    return (x * lax.rsqrt(ms + EPS)) * (g * (1.0 + scale)) + shift


def _mod_kernel(c_ref, w_ref, b_ref, o_ref):
    s = _silu(c_ref[...]).astype(BF16)
    o_ref[...] = _dot(s, w_ref[...].astype(BF16)) + b_ref[...]


def _modulation(cvec, w_mod, b_mod):
    nblk = N_MOD
    out = pl.pallas_call(
        _mod_kernel,
        grid=(DEPTH, nblk),
        in_specs=[
            pl.BlockSpec((MOD_ROWS, D), lambda l, j: (0, 0)),
            pl.BlockSpec((None, D, D), lambda l, j: (l, 0, j)),
            pl.BlockSpec((None, 1, D), lambda l, j: (l, 0, j)),
        ],
        out_specs=pl.BlockSpec((None, MOD_ROWS, D), lambda l, j: (l, 0, j)),
        out_shape=jax.ShapeDtypeStruct((DEPTH, MOD_ROWS, N_MOD * D), F32),
        compiler_params=_params(2),
        name="modulation",
    )(cvec, w_mod, b_mod.reshape(DEPTH, 1, N_MOD * D))
    return out.reshape(DEPTH, MOD_ROWS, N_MOD, D)


def _mod_spec(layer, is_ctx, tile):
    tiles_per_batch = SEQ // tile
    if is_ctx:
        return pl.BlockSpec((None, None, N_MOD, D), lambda t: (layer, 0, 0, 0))
    return pl.BlockSpec((None, None, N_MOD, D), lambda t: (layer, 1 + t // tiles_per_batch, 0, 0))


def _norm_spec(layer):
    return pl.BlockSpec((None, 3, D), lambda t: (layer, 0, 0))


def _row_spec(width, tile):
    return pl.BlockSpec((tile, width), lambda t: (t, 0))


def _rows_kernel(sub, pre_mix, post, *refs):
    refs = list(refs)
    h_ref, mod_ref, g_ref = refs[:3]
    del refs[:3]
    if pre_mix:
        x_ref, wmix_ref = refs[:2]
        del refs[:2]
    w13_ref, w2_ref = refs[:2]
    del refs[:2]
    if post in ("ab", "final"):
        wpost_ref = refs.pop(0)
    o_ref = refs.pop(0)

    h = h_ref[...]
    if pre_mix:
        h = h + mod_ref[5:6, :] * _dot(x_ref[...], wmix_ref[...])
    shift = mod_ref[3 * sub:3 * sub + 1, :]
    scale = mod_ref[3 * sub + 1:3 * sub + 2, :]
    gate = mod_ref[3 * sub + 2:3 * sub + 3, :]
    xb = _adaln(h, g_ref[sub:sub + 1, :], shift, scale).astype(BF16)
    acc = jnp.zeros(h.shape, F32)
    for c in range(D_FF // FF_CHUNK):
        lo = c * FF_CHUNK
        a = _dot(xb, w13_ref[:, lo:lo + FF_CHUNK])
        b = _dot(xb, w13_ref[:, D_FF + lo:D_FF + lo + FF_CHUNK])
        act = (_silu(a) * b).astype(BF16)
        acc = acc + _dot(act, w2_ref[lo:lo + FF_CHUNK, :])
    y = h + (0.5 * gate) * acc
    if post == "final":
        ms = jnp.mean(y * y, axis=-1, keepdims=True)
        y = (y * lax.rsqrt(ms + EPS)) * wpost_ref[...]
    o_ref[...] = y
    if post in ("ab", "xn"):
        xm = _adaln(y, g_ref[1:2, :], mod_ref[3:4, :], mod_ref[4:5, :]).astype(BF16)
    if post == "xn":
        refs[0][...] = xm
    if post == "ab":
        q_ref, k_ref, v_ref, u_ref = refs
        q_ref[...] = (_dot(xm, wpost_ref[:, 0:A_WIDTH]) * (A_DH ** -0.5)).astype(BF16)
        k_ref[...] = _dot(xm, wpost_ref[:, A_WIDTH:2 * A_WIDTH]).astype(BF16)
        v_ref[...] = _dot(xm, wpost_ref[:, 2 * A_WIDTH:3 * A_WIDTH]).astype(BF16)
        u_ref[...] = _dot(xm, wpost_ref[:, 3 * A_WIDTH:])


def _rows(h, mod, norm_g, w13, w2, layer, sub, is_ctx, mix=None, post=None, w_post=None):
    n = h.shape[0]
    which = sub // 2
    tile = ROW_TILE_AB if post == "ab" else ROW_TILE
    row_spec = functools.partial(_row_spec, tile=tile)
    in_specs = [row_spec(D), _mod_spec(layer, is_ctx, tile), _norm_spec(layer)]
    args = [h, mod, norm_g]
    if mix is not None:
        in_specs += [row_spec(D), _const_spec((D, D))]
        args += list(mix)
    in_specs += [
        pl.BlockSpec((None, None, D, 2 * D_FF), lambda t: (layer, which, 0, 0), pipeline_mode=pl.Buffered(1)),
        pl.BlockSpec((None, None, D_FF, D), lambda t: (layer, which, 0, 0), pipeline_mode=pl.Buffered(1)),
    ]
    args += [w13, w2]
    out_specs = [row_spec(D)]
    out_shape = [jax.ShapeDtypeStruct((n, D), F32)]
    if post == "final":
        in_specs.append(_const_spec((1, D)))
        args.append(w_post.reshape(1, D))
    elif post == "ab":
        in_specs.append(_const_spec((D, 3 * A_WIDTH + B_WIDTH)))
        args.append(w_post)
        out_specs += [row_spec(A_WIDTH)] * 3 + [row_spec(B_WIDTH)]
        out_shape += [jax.ShapeDtypeStruct((n, A_WIDTH), BF16)] * 3 + [jax.ShapeDtypeStruct((n, B_WIDTH), F32)]
    elif post == "xn":
        out_specs.append(row_spec(D))
        out_shape.append(jax.ShapeDtypeStruct((n, D), BF16))
    return pl.pallas_call(
        functools.partial(_rows_kernel, sub, mix is not None, post),
        grid=(n // tile,),
        in_specs=in_specs,
        out_specs=out_specs,
        out_shape=out_shape,
        compiler_params=_params(1),
        name="rows",
    )(*args)


def _attn_bias_table(rpb):
    col = jnp.arange(GRID_W)
    win_c0 = jnp.clip(col - WIN_C // 2, 0, GRID_W - WIN_C)
    kc = jnp.arange(GRID_W)
    col_ok = (kc[None, :] >= win_c0[:, None]) & (kc[None, :] < win_c0[:, None] + WIN_C)
    rel_c = jnp.clip(kc[None, :] - col[:, None], -(WIN_C - 1), WIN_C - 1) + (WIN_C - 1)
    onehot = (rel_c[None] == jnp.arange(2 * WIN_C - 1)[:, None, None]).astype(F32)
    by_col = jnp.einsum("hri,iqk->hqrk", rpb.astype(F32), onehot, precision=lax.Precision.HIGHEST)
    by_col = jnp.where(col_ok[None, :, None, :], by_col, NEG_BIG)
    tab = jnp.stack([by_col[:, :, WIN_R - 1 - dr:2 * WIN_R - 1 - dr] for dr in range(WIN_R)])
    return tab.reshape(WIN_R, A_HEADS * GRID_W, WIN_R * GRID_W)


def _softmax_parts(s_parts):
    m = s_parts[0].max(axis=-1, keepdims=True)
    for s in s_parts[1:]:
        m = jnp.maximum(m, s.max(axis=-1, keepdims=True))
    ps = [jnp.exp(s - m) for s in s_parts]
    den = ps[0].sum(axis=-1, keepdims=True)
    for p in ps[1:]:
        den = den + p.sum(axis=-1, keepdims=True)
    return [p.astype(BF16) for p in ps], 1.0 / den


def _stack_heads(x2, first):
    zero = jnp.zeros((), x2.dtype)
    return jnp.concatenate([jnp.where(first, x2, zero), jnp.where(first, zero, x2)], axis=0)


def _unstack_heads(o, first):
    m = o.shape[0] // 2
    return jnp.where(first, o[:m], o[m:])


def _shift_rows(x, k, row, n):
    if k > 0:
        return jnp.where(row >= k, pltpu.roll(x, k, 0), 0.0)
    return jnp.where(row < n + k, pltpu.roll(x, n + k, 0), 0.0)


def _pool_mix(u_ref, pw_ref, ps_ref, o_ref, n):
    row = lax.broadcasted_iota(jnp.int32, (n, B_GROUP), 0)
    rowf = row.astype(F32)
    for gi, w in enumerate(POOL_WINDOWS):
        x = u_ref[:, gi * B_GROUP:(gi + 1) * B_GROUP]
        half = w // 2
        back = x
        fwd = x
        m = 1
        while m < half:
            back = back + _shift_rows(back, m, row, n)
            fwd = fwd + _shift_rows(fwd, -m, row, n)
            m *= 2
        total = _shift_rows(back, 1, row, n) + fwd
        cnt = jnp.minimum(rowf + half, float(n)) - jnp.maximum(rowf - half, 0.0)
        d = (total / cnt - x).astype(BF16)
        y = _dot(d, pw_ref[gi]) * ps_ref[:, gi * B_GROUP:(gi + 1) * B_GROUP]
        o_ref[:, A_WIDTH + gi * B_GROUP:A_WIDTH + (gi + 1) * B_GROUP] = y.astype(o_ref.dtype)


def _ab_core_kernel(qc_ref, kc_ref, vc_ref, uc_ref, ql_ref, kl_ref, vl_ref, ul_ref, bias_ref, pw_ref, ps_ref,
                    oc_ref, ol_ref, qs_s, sc_s, pc_s, ol_s, inv_s):
    lane = lax.broadcasted_iota(jnp.int32, (1, 2 * A_DH), 1)
    first = lane < A_DH
    pairs = [slice(p * 2 * A_DH, (p + 1) * 2 * A_DH) for p in range(A_HEADS // 2)]

    qs = [_stack_heads(qc_ref[:, cols], first) for cols in pairs]
    s_all = jnp.concatenate([_dot_nt(q, kc_ref[:, cols]) for q, cols in zip(qs, pairs)], axis=0)
    (p_all,), inv = _softmax_parts([s_all])
    for i, cols in enumerate(pairs):
        rows = slice(i * 2 * CTX, (i + 1) * 2 * CTX)
        o = _dot(p_all[rows], vc_ref[:, cols]) * inv[rows]
        oc_ref[:, cols] = _unstack_heads(o, first).astype(oc_ref.dtype)

    n_pairs = len(pairs)
    blk_q = ROW_BLOCK * GRID_W
    per_row = 2 * GRID_W

    def block_body(blk, carry):
        q0 = pl.multiple_of(blk * blk_q, blk_q)
        for i, cols in enumerate(pairs):
            for j in range(ROW_BLOCK):
                q = _stack_heads(ql_ref[pl.ds(q0 + j * GRID_W, GRID_W), cols], first)
                qs_s[i, j * per_row:(j + 1) * per_row, :] = q
            sc_s[i] = _dot_nt(qs_s[i], kc_ref[:, cols])

        def row_body(j, carry):
            r = blk * ROW_BLOCK + j
            r0 = jnp.clip(r - WIN_R // 2, 0, ROWS - WIN_R)
            krows = pl.ds(pl.multiple_of(r0 * GRID_W, GRID_W), WIN_R * GRID_W)
            rows_j = pl.ds(pl.multiple_of(j * per_row, per_row), per_row)
            s_loc = jnp.concatenate([_dot_nt(qs_s[i, rows_j, :], kl_ref[krows, cols])
                                     for i, cols in enumerate(pairs)], axis=0)
            s_ctx = jnp.concatenate([sc_s[i, rows_j, :] for i in range(n_pairs)], axis=0)
            (p_loc, p_ctx), inv = _softmax_parts([s_loc + bias_ref[r - r0], s_ctx])
            for i, cols in enumerate(pairs):
                rows = slice(i * per_row, (i + 1) * per_row)
                pc_s[i, rows_j, :] = p_ctx[rows]
                ol_s[i, rows_j, :] = _dot(p_loc[rows], vl_ref[krows, cols])
                inv_s[i, rows_j, :] = inv[rows]
            return carry

        lax.fori_loop(0, ROW_BLOCK, row_body, 0, unroll=4)

        for i, cols in enumerate(pairs):
            o = (ol_s[i] + _dot(pc_s[i], vc_ref[:, cols])) * inv_s[i]
            for j in range(ROW_BLOCK):
                ol_ref[pl.ds(q0 + j * GRID_W, GRID_W), cols] = _unstack_heads(
                    o[j * per_row:(j + 1) * per_row], first).astype(ol_ref.dtype)
        return carry

    lax.fori_loop(0, ROWS // ROW_BLOCK, block_body, 0, unroll=2)

    _pool_mix(uc_ref, pw_ref, ps_ref, oc_ref, CTX)
    _pool_mix(ul_ref, pw_ref, ps_ref, ol_ref, SEQ)


def _ab_core(qc, kc, vc, uc, ql, kl, vl, ul, bias, pool_w, pool_scale):
    cspec = lambda w: pl.BlockSpec((CTX, w), lambda b: (b, 0))
    lspec = lambda w: pl.BlockSpec((SEQ, w), lambda b: (b, 0))
    return pl.pallas_call(
        _ab_core_kernel,
        grid=(BATCH,),
        in_specs=[
            cspec(A_WIDTH), cspec(A_WIDTH), cspec(A_WIDTH), cspec(B_WIDTH),
            lspec(A_WIDTH), lspec(A_WIDTH), lspec(A_WIDTH), lspec(B_WIDTH),
            _const_spec((WIN_R, A_HEADS * GRID_W, WIN_R * GRID_W)),
            _const_spec((len(POOL_WINDOWS), B_GROUP, B_GROUP)),
            _const_spec((1, B_WIDTH)),
        ],
        out_specs=[cspec(D), lspec(D)],
        out_shape=[
            jax.ShapeDtypeStruct((BATCH * CTX, D), BF16),
            jax.ShapeDtypeStruct((BATCH * SEQ, D), BF16),
        ],
        scratch_shapes=[
            pltpu.VMEM((A_HEADS // 2, 2 * ROW_BLOCK * GRID_W, 2 * A_DH), BF16),
            pltpu.VMEM((A_HEADS // 2, 2 * ROW_BLOCK * GRID_W, CTX), F32),
            pltpu.VMEM((A_HEADS // 2, 2 * ROW_BLOCK * GRID_W, CTX), BF16),
            pltpu.VMEM((A_HEADS // 2, 2 * ROW_BLOCK * GRID_W, 2 * A_DH), F32),
            pltpu.VMEM((A_HEADS // 2, 2 * ROW_BLOCK * GRID_W, 1), F32),
        ],
        compiler_params=_params(1),
        name="ab_core",
    )(qc, kc, vc, uc, ql, kl, vl, ul, bias, pool_w, pool_scale)


HG_SMALL_LEVELS = (2, 4)


def _hg_constants():
    t = jnp.arange(HG_CHUNK)
    low = (t[:, None] >= t[None, :]).astype(BF16)
    cum = jnp.stack([jnp.concatenate([low, low], axis=1), jnp.concatenate([low.T, low.T], axis=1)])
    upper = jnp.stack([(t & c) != 0 for c in HG_SMALL_LEVELS])
    sgn_f = jnp.where(upper, 1.0, -1.0).astype(F32)
    sgn = jnp.stack([sgn_f, -sgn_f])[:, :, :, None]
    return cum, jnp.broadcast_to(sgn, (2, len(HG_SMALL_LEVELS), HG_CHUNK, HG_DK))


def _bcast_row(x3, k):
    return jnp.broadcast_to(x3[:, k:k + 1, :], x3.shape)


def _hg_scores(q, k, g, sgn_ref, level_of, reverse):
    n8, n16 = HG_CHUNK // 8, HG_CHUNK // 16
    row = lax.broadcasted_iota(jnp.int32, (HG_CHUNK, HG_CHUNK), 0)
    colm = lax.broadcasted_iota(jnp.int32, (HG_CHUNK, HG_CHUNK), 1)
    a = jnp.where(row == colm, jnp.sum(q * k, axis=-1, keepdims=True), 0.0)
    qb = q.astype(BF16)
    kb = k.astype(BF16)
    a = jnp.where(level_of == 0, _dot_nt(qb * (1.0 - k).astype(BF16), kb), a)

    g3 = g.reshape(n8, 8, HG_DK)
    sub = lax.broadcasted_iota(jnp.int32, g3.shape, 1)
    pick = (lambda r: r + 1) if reverse else (lambda r: r)
    small_refs = (jnp.where(sub < 4, _bcast_row(g3, pick(1)), _bcast_row(g3, pick(5))), _bcast_row(g3, pick(3)))
    for i, ref in enumerate(small_refs):
        w = jnp.exp2((g - ref.reshape(HG_CHUNK, HG_DK)) * sgn_ref[i]).astype(BF16)
        a = jnp.where(level_of == i + 1, _dot_nt(qb * w, kb * w), a)

    edge = _bcast_row(g3, 0 if reverse else 7)
    qb3, kb3 = qb.reshape(n16, 16, HG_DK), kb.reshape(n16, 16, HG_DK)
    lvl3 = level_of.reshape(n16, 16, HG_CHUNK)
    for lvl, c in [(l, 1 << l) for l in range(3, HG_CHUNK.bit_length() - 1)]:
        span = 2 * c // 8
        dist = []
        for j in range(n8):
            b = edge[(j // span) * span + span // 2 - (0 if reverse else 1)]
            upper = j % span >= span // 2
            dist.append(g3[j] - b if upper != reverse else b - g3[j])
        w = jnp.exp2(jnp.concatenate(dist, axis=0)).astype(BF16)
        if c == 8:
            a = jnp.where(level_of == lvl, _dot_nt(qb * w, kb * w), a)
            continue
        w3 = w.reshape(n16, 16, HG_DK)
        span16 = span // 2
        is_query = [(j % span16 >= span16 // 2) != reverse for j in range(n16)]
        lhs = jnp.concatenate([qb3[j] * w3[j] for j in range(n16) if is_query[j]], axis=0)
        rhs = jnp.concatenate([kb3[j] if is_query[j] else kb3[j] * w3[j] for j in range(n16)], axis=0)
        a_l = _dot_nt(lhs, rhs).reshape(n16 // 2, 16, HG_CHUNK)
        a3 = a.reshape(n16, 16, HG_CHUNK)
        groups, u = [], 0
        for j in range(n16):
            if is_query[j]:
                groups.append(jnp.where(lvl3[j] == lvl, a_l[u], a3[j]))
                u += 1
            else:
                groups.append(a3[j])
        a = jnp.concatenate(groups, axis=0)
    return a.astype(BF16)


def _hg_state(k, v, g, st, reverse):
    g_end = g[0:1, :] if reverse else g[HG_CHUNK - 1:HG_CHUNK, :]
    kp = (k * jnp.exp2(g_end - g)).astype(BF16)
    return st * jnp.exp2(g_end) + _dot_tn(v.astype(BF16), kp)


def _hg_out(q, v, g, st, a):
    qp = (q * jnp.exp2(g)).astype(BF16)
    return _dot_nt(qp, st.astype(BF16)) + _dot(a, v.astype(BF16))


def _hg_kernel(xc_ref, xl_ref, wq_ref, wi_ref, wff_ref, wfb_ref, wg_ref, lb_ref, gn_ref, cum_ref, sgn_ref, o_ref,
               q_s, v_s, kf_s, kb_s, gf_s, gb_s, of_s, ob_s, a_s):
    n_ctx_chunks = CTX // HG_CHUNK
    n_lat_chunks = SEQ // HG_CHUNK
    heads = range(HG_STEP_HEADS)
    dirs = ((False, kf_s, gf_s, of_s), (True, kb_s, gb_s, ob_s))

    def project(x_ref, base, n):
        x = x_ref[...]
        pq, pv, pff, pfb, pg = (_dot(x, w_ref[...]) for w_ref in (wq_ref, wi_ref, wff_ref, wfb_ref, wg_ref))
        rows = slice(base, base + n)
        gates = []
        for hh in heads:
            cols = slice(hh * HG_DK, (hh + 1) * HG_DK)
            q_s[hh, rows, :] = _silu(pq[:, cols])
            v_s[hh, rows, :] = pv[:, cols]
            for d, (_, k_s, g_s, _) in enumerate(dirs):
                lbd = lb_ref[d:d + 1, cols]
                f = lbd + (1.0 - lbd) * _sigmoid((pff, pfb)[d][:, cols])
                k_s[hh, rows, :] = 1.0 - f
                lf = jnp.log2(f)
                hi = lf.astype(BF16)
                lo = (lf - hi.astype(F32)).astype(BF16)
                for i in range(n // HG_CHUNK):
                    blk = slice(i * HG_CHUNK, (i + 1) * HG_CHUNK)
                    g_s[hh, base + i * HG_CHUNK:base + (i + 1) * HG_CHUNK, :] = _dot(
                        cum_ref[d], jnp.concatenate([hi[blk], lo[blk]], axis=0))
            gates.append(pg[:, cols])
        return gates

    project(xc_ref, 0, CTX)
    gates = project(xl_ref, CTX, SEQ)

    row = lax.broadcasted_iota(jnp.int32, (HG_CHUNK, HG_CHUNK), 0)
    colm = lax.broadcasted_iota(jnp.int32, (HG_CHUNK, HG_CHUNK), 1)
    diff = row ^ colm
    lvl = jnp.zeros_like(diff)
    for c in range(1, HG_CHUNK.bit_length() - 1):
        lvl = lvl + (diff >= (1 << c)).astype(jnp.int32)
    lvl_of = (jnp.where(row > colm, lvl, -1), jnp.where(row < colm, lvl, -1))

    def operands(hh, d, base):
        rows = pl.ds(base, HG_CHUNK)
        _, k_s, g_s, _ = dirs[d]
        return q_s[hh, rows, :], k_s[hh, rows, :], v_s[hh, rows, :], g_s[hh, rows, :]

    sts = [jnp.zeros((HG_DK, HG_DK), F32) for _ in heads for _ in dirs]
    for i in range(n_ctx_chunks):
        for hh in heads:
            for d, (reverse, *_) in enumerate(dirs):
                j = (n_ctx_chunks - 1 - i) if reverse else i
                _, k, v, g = operands(hh, d, j * HG_CHUNK)
                sts[2 * hh + d] = _hg_state(k, v, g, sts[2 * hh + d], reverse)

    def lat_base(i, reverse):
        j = (n_lat_chunks - 1 - i) if reverse else i
        return pl.multiple_of(j * HG_CHUNK, HG_CHUNK)

    def scores(i):
        for hh in heads:
            for d, (reverse, *_) in enumerate(dirs):
                q, k, _, g = operands(hh, d, CTX + lat_base(i, reverse))
                a_s[hh, d, i] = _hg_scores(q, k, g, sgn_ref.at[d], lvl_of[d], reverse)

    def body(i, carry):
        sts = list(carry)
        for hh in heads:
            for d, (reverse, _, _, o_s) in enumerate(dirs):
                base = lat_base(i, reverse)
                q, k, v, g = operands(hh, d, CTX + base)
                o_s[hh, pl.ds(base, HG_CHUNK), :] = _hg_out(q, v, g, sts[2 * hh + d], a_s[hh, d, i])
                sts[2 * hh + d] = _hg_state(k, v, g, sts[2 * hh + d], reverse)
        scores(jnp.minimum(i + 1, n_lat_chunks - 1))
        return tuple(sts)

    scores(0)
    lax.fori_loop(0, n_lat_chunks, body, tuple(sts), unroll=16)

    for hh in heads:
        o = of_s[hh] + ob_s[hh]
        o = o * lax.rsqrt(jnp.mean(o * o, axis=-1, keepdims=True) + EPS) * gn_ref[...]
        o_ref[:, hh * HG_DK:(hh + 1) * HG_DK] = (o * _silu(gates[hh])).astype(o_ref.dtype)


def _hg_core(xn_ctx, xn_lat, w_in, lb, gnorm):
    nh = HG_STEP_HEADS
    steps = HG_HEADS // nh
    seq_scr = pltpu.VMEM((nh, CTX + SEQ, HG_DK), F32)
    lat_scr = pltpu.VMEM((nh, SEQ, HG_DK), F32)
    a_scr = pltpu.VMEM((nh, 2, SEQ // HG_CHUNK, HG_CHUNK, HG_CHUNK), BF16)
    cum, sgn = _hg_constants()
    return pl.pallas_call(
        _hg_kernel,
        grid=(BATCH, steps),
        in_specs=[
            pl.BlockSpec((CTX, D), lambda b, h: (b, 0)),
            pl.BlockSpec((SEQ, D), lambda b, h: (b, 0)),
            *[pl.BlockSpec((D, nh * HG_DK), lambda b, h, j=j: (0, j * steps + h)) for j in range(5)],
            pl.BlockSpec((2, nh * HG_DK), lambda b, h: (0, h)),
            pl.BlockSpec((1, HG_DK), lambda b, h: (0, 0)),
            _const_spec(cum.shape),
            _const_spec(sgn.shape),
        ],
        out_specs=pl.BlockSpec((SEQ, nh * HG_DK), lambda b, h: (b, h)),
        out_shape=jax.ShapeDtypeStruct((BATCH * SEQ, HG_KDIM), BF16),
        scratch_shapes=[seq_scr] * 6 + [lat_scr] * 2 + [a_scr],
        compiler_params=_params(2),
        name="hgrn2",
    )(xn_ctx, xn_lat, *([w_in] * 5), lb, gnorm, cum, sgn)


def kernel(x, c, ctx, c_ctx, w_mod, b_mod, norm_g, ffn_w13, ffn_w2, ab_w_in, ab_rpb, ab_pool_w, ab_pool_scale,
           ab_w_out, hg_w_in, hg_lb_logits, hg_gnorm, hg_w_out, final_g):
    h_lat = x.reshape(BATCH * SEQ, D)
    h_ctx = ctx.reshape(BATCH * CTX, D)

    cvec = jnp.zeros((MOD_ROWS, D), F32).at[0].set(c_ctx).at[1:1 + BATCH].set(c)
    mod = _modulation(cvec, w_mod, b_mod)

    w13 = ffn_w13.astype(BF16)
    w2 = ffn_w2.astype(BF16)
    ffn = functools.partial(_rows, mod=mod, norm_g=norm_g, w13=w13, w2=w2)

    w_in = ab_w_in[0].astype(BF16)
    h_lat, ql, kl, vl, ul = ffn(h_lat, layer=0, sub=0, is_ctx=False, post="ab", w_post=w_in)
    h_ctx, qc, kc, vc, uc = ffn(h_ctx, layer=0, sub=0, is_ctx=True, post="ab", w_post=w_in)
    mix_ctx, mix_lat = _ab_core(qc, kc, vc, uc, ql, kl, vl, ul, _attn_bias_table(ab_rpb[0]),
                                ab_pool_w[0].astype(BF16), ab_pool_scale[0].reshape(1, B_WIDTH))
    w_out = ab_w_out[0].astype(BF16)
    (h_lat,) = ffn(h_lat, layer=0, sub=2, is_ctx=False, mix=(mix_lat, w_out))
    (h_ctx,) = ffn(h_ctx, layer=0, sub=2, is_ctx=True, mix=(mix_ctx, w_out))

    h_lat, xn_lat = ffn(h_lat, layer=1, sub=0, is_ctx=False, post="xn")
    _, xn_ctx = ffn(h_ctx, layer=1, sub=0, is_ctx=True, post="xn")
    sm = jax.nn.softmax(hg_lb_logits.astype(F32), axis=0)
    lb = (jnp.cumsum(sm, axis=0) - sm[0:1])[1]
    o = _hg_core(xn_ctx, xn_lat, hg_w_in[0].astype(BF16), lb, hg_gnorm[0].reshape(1, HG_DK))
    (out,) = ffn(h_lat, layer=1, sub=2, is_ctx=False, mix=(o, hg_w_out[0].astype(BF16)), post="final",
                 w_post=final_g)
    return out.reshape(BATCH, SEQ, D)
```
